```python
import jax, jax.numpy as jnp
from jax import lax
import numpy as np

D_MODEL = 2048
BATCH = 4
SEQ = 4096
DEPTH = 1

GLA_WIDTH = D_MODEL // 2
ATT_WIDTH = D_MODEL - GLA_WIDTH
MIX_WIDTH = GLA_WIDTH + ATT_WIDTH

GLA_HEADS = 4
GLA_DV = GLA_WIDTH // GLA_HEADS
GLA_DK = GLA_DV // 2
GLA_QK = GLA_HEADS * GLA_DK
GLA_GATE_RANK = 16
GLA_GATE_TAU = 16.0
GLA_CHUNK = 64

ATT_HD = 128
ATT_HEADS = ATT_WIDTH // ATT_HD
DIL_CONFIGS = ((128, 1), (512, 4), (2048, 16))
ATT_BLOCK = 128
ROPE_THETA = 10000.0

D_FF = 5632
FFN_RESIDUAL_WEIGHT = 0.5
EPS = 1e-6

IN_SIZES = (GLA_QK, GLA_QK, GLA_WIDTH, GLA_GATE_RANK, GLA_WIDTH, ATT_WIDTH, ATT_WIDTH, ATT_WIDTH)
D_IN = sum(IN_SIZES)

kernel_name = "hymba_gla_dilated_macaron_layer"


def rms_norm(x, g):
    xf = x.astype(jnp.float32)
    y = xf * lax.rsqrt(jnp.mean(xf * xf, axis=-1, keepdims=True) + EPS)
    return (y * g.astype(jnp.float32)).astype(x.dtype)


def swiglu(x, w_gate, w_up, w_down):
    return (jax.nn.silu(x @ w_gate) * (x @ w_up)) @ w_down


def rope(x, pos):
    half = x.shape[-1] // 2
    inv_freq = 1.0 / (ROPE_THETA ** (jnp.arange(half, dtype=jnp.float32) / half))
    ang = pos.astype(jnp.float32)[:, None] * inv_freq[None, :]
    cos = jnp.cos(ang)[None, :, None, :]
    sin = jnp.sin(ang)[None, :, None, :]
    xf = x.astype(jnp.float32)
    x1, x2 = xf[..., :half], xf[..., half:]
    out = jnp.concatenate([x1 * cos - x2 * sin, x2 * cos + x1 * sin], axis=-1)
    return out.astype(x.dtype)


def gla_chunked(q, k, v, log_a):
    B, S, H, DK = q.shape
    DV = v.shape[-1]
    C = GLA_CHUNK
    n = S // C

    def chunks(t):
        return t.astype(jnp.float32).reshape(B, n, C, H, t.shape[-1]).transpose(1, 0, 3, 2, 4)

    qc = chunks(q) * (DK ** -0.5)
    kc = chunks(k)
    vc = chunks(v)
    b = jnp.cumsum(chunks(log_a), axis=3)
    b_ref = b[:, :, :, C // 2 - 1:C // 2, :]
    b_last = b[:, :, :, C - 1:, :]

    scores = jnp.einsum('nbhid,nbhjd->nbhij', qc * jnp.exp(b - b_ref), kc * jnp.exp(b_ref - b))
    causal = jnp.tril(jnp.ones((C, C), dtype=bool))
    o_intra = jnp.einsum('nbhij,nbhjv->nbhiv', jnp.where(causal, scores, 0.0), vc)

    q_inter = qc * jnp.exp(b)
    k_state = kc * jnp.exp(b_last - b)
    decay = jnp.exp(b_last[:, :, :, 0, :])

    def step(state, xs):
        q_i, k_i, v_i, d_i = xs
        o = jnp.einsum('bhcd,bhdv->bhcv', q_i, state)
        state = d_i[..., None] * state + jnp.einsum('bhcd,bhcv->bhdv', k_i, v_i)
        return state, o

    state0 = jnp.zeros((B, H, DK, DV), jnp.float32)
    _, o_inter = lax.scan(step, state0, (q_inter, k_state, vc, decay))
    o = o_intra + o_inter
    return o.transpose(1, 0, 3, 2, 4).reshape(B, S, H, DV)


def dilated_branch(q, k, v, window, dilation):
    B, S, H, D = q.shape
    r = dilation
    L = S // r
    steps = window // dilation
    blk = ATT_BLOCK
    nb = -(-L // blk)
    Lp = nb * blk

    def to_blocks(t):
        t = t.reshape(B, L, r, H, D).transpose(0, 2, 1, 3, 4)
        t = jnp.pad(t, ((0, 0), (0, 0), (0, Lp - L), (0, 0), (0, 0)))
        return t.reshape(B, r, nb, blk, H, D)

    qb, kb, vb = to_blocks(q), to_blocks(k), to_blocks(v)

    def with_prev(t):
        prev = jnp.concatenate([jnp.zeros_like(t[:, :, :1]), t[:, :, :-1]], axis=2)
        return jnp.concatenate([prev, t], axis=3)

    kk, vv = with_prev(kb), with_prev(vb)
    s = jnp.einsum('brnqhd,brnkhd->brnhqk', qb.astype(jnp.float32), kk.astype(jnp.float32)) * (D ** -0.5)

    qi = jnp.arange(blk)[:, None] + blk
    kj = jnp.arange(2 * blk)[None, :]
    dist = qi - kj
    band = (dist >= 0) & (dist <= steps)
    first = (jnp.arange(nb) == 0)[:, None, None]
    mask = band[None] & ~(first & (kj < blk)[None])
    s = jnp.where(mask[None, None, :, None, :, :], s, -jnp.inf)

    m = jnp.max(s, axis=-1, keepdims=True)
    p = jnp.exp(s - m)
    l = jnp.sum(p, axis=-1, keepdims=True)
    o = jnp.einsum('brnhqk,brnkhd->brnqhd', p / l, vv.astype(jnp.float32))
    lse = (m + jnp.log(l))[..., 0]

    o = o.reshape(B, r, Lp, H, D)[:, :, :L].transpose(0, 2, 1, 3, 4).reshape(B, S, H, D)
    lse = lse.transpose(0, 1, 2, 4, 3).reshape(B, r, Lp, H)[:, :, :L]
    lse = lse.transpose(0, 2, 1, 3).reshape(B, S, H)
    return o, lse


def dilated_attention(q, k, v):
    outs, lses = [], []
    for window, dilation in DIL_CONFIGS:
        o_i, lse_i = dilated_branch(q, k, v, window, dilation)
        outs.append(o_i)
        lses.append(lse_i)
    w = jax.nn.softmax(jnp.stack(lses, axis=0), axis=0)
    return jnp.sum(w[..., None] * jnp.stack(outs, axis=0), axis=0)


def setup_inputs(seed: int = 0) -> dict:
    key = jax.random.key(seed)
    ks = jax.random.split(key, 20)
    f32 = jnp.float32

    def w(k_, shape, fan_in):
        return jax.random.normal(k_, shape, f32) * (fan_in ** -0.5)

    def gain(k_, n):
        return 1.0 + 0.02 * jax.random.normal(k_, (DEPTH, n), f32)

    return {
        "x": jax.random.normal(ks[0], (BATCH, SEQ, D_MODEL), f32),
        "ffn1_norm": gain(ks[1], D_MODEL),
        "ffn1_w_gate": w(ks[2], (DEPTH, D_MODEL, D_FF), D_MODEL),
        "ffn1_w_up": w(ks[3], (DEPTH, D_MODEL, D_FF), D_MODEL),
        "ffn1_w_down": w(ks[4], (DEPTH, D_FF, D_MODEL), D_FF),
        "mix_norm": gain(ks[5], D_MODEL),
        "w_in": w(ks[6], (DEPTH, D_MODEL, D_IN), D_MODEL),
        "gla_gate_up": w(ks[7], (DEPTH, GLA_GATE_RANK, GLA_QK), GLA_GATE_RANK),
        "gla_gate_bias": 0.1 * jax.random.normal(ks[8], (DEPTH, GLA_QK), f32),
        "gla_out_norm": gain(ks[9], GLA_DV),
        "att_q_norm": gain(ks[10], ATT_HD),
        "att_k_norm": gain(ks[11], ATT_HD),
        "w_out": w(ks[12], (DEPTH, MIX_WIDTH, D_MODEL), MIX_WIDTH),
        "ffn2_norm": gain(ks[13], D_MODEL),
        "ffn2_w_gate": w(ks[14], (DEPTH, D_MODEL, D_FF), D_MODEL),
        "ffn2_w_up": w(ks[15], (DEPTH, D_MODEL, D_FF), D_MODEL),
        "ffn2_w_down": w(ks[16], (DEPTH, D_FF, D_MODEL), D_FF),
    }


def reference(x, ffn1_norm, ffn1_w_gate, ffn1_w_up, ffn1_w_down, mix_norm, w_in,
              gla_gate_up, gla_gate_bias, gla_out_norm, att_q_norm, att_k_norm, w_out,
              ffn2_norm, ffn2_w_gate, ffn2_w_up, ffn2_w_down):
    B, S, _ = x.shape
    pos = jnp.arange(S)
    split_points = list(np.cumsum(IN_SIZES)[:-1])
    for l in range(DEPTH):
        x = x + FFN_RESIDUAL_WEIGHT * swiglu(rms_norm(x, ffn1_norm[l]), ffn1_w_gate[l], ffn1_w_up[l], ffn1_w_down[l])

        h = rms_norm(x, mix_norm[l])
        z = h @ w_in[l]
        q_g, k_g, v_g, g_low, r_g, q_a, k_a, v_a = jnp.split(z, split_points, axis=-1)

        gate_logit = (g_low @ gla_gate_up[l] + gla_gate_bias[l]).astype(jnp.float32)
        log_a = jax.nn.log_sigmoid(gate_logit) / GLA_GATE_TAU
        o_g = gla_chunked(q_g.reshape(B, S, GLA_HEADS, GLA_DK),
                          k_g.reshape(B, S, GLA_HEADS, GLA_DK),
                          v_g.reshape(B, S, GLA_HEADS, GLA_DV),
                          log_a.reshape(B, S, GLA_HEADS, GLA_DK))
        o_g = rms_norm(o_g, gla_out_norm[l]).reshape(B, S, GLA_WIDTH)
        o_g = (o_g * jax.nn.silu(r_g.astype(jnp.float32))).astype(x.dtype)

        qa = rope(rms_norm(q_a.reshape(B, S, ATT_HEADS, ATT_HD), att_q_norm[l]), pos)
        ka = rope(rms_norm(k_a.reshape(B, S, ATT_HEADS, ATT_HD), att_k_norm[l]), pos)
        va = v_a.reshape(B, S, ATT_HEADS, ATT_HD)
        o_a = dilated_attention(qa, ka, va).reshape(B, S, ATT_WIDTH).astype(x.dtype)

        x = x + jnp.concatenate([o_g, o_a], axis=-1) @ w_out[l]

        x = x + FFN_RESIDUAL_WEIGHT * swiglu(rms_norm(x, ffn2_norm[l]), ffn2_w_gate[l], ffn2_w_up[l], ffn2_w_down[l])
    return x
```

```python
import functools

import numpy as np
import jax
import jax.numpy as jnp
from jax import lax
from jax.experimental import pallas as pl
from jax.experimental.pallas import tpu as pltpu

F32 = jnp.float32
BF16 = jnp.bfloat16

D_MODEL = 2048
D_FF = 5632
EPS = 1e-6
FFN_RESIDUAL_WEIGHT = 0.5

GLA_HEADS = 4
GLA_DK = 128
GLA_DV = 256
GLA_QK = GLA_HEADS * GLA_DK
GLA_WIDTH = GLA_HEADS * GLA_DV
GLA_GATE_RANK = 16
GLA_GATE_TAU = 16.0
GLA_CHUNK = 64

ATT_HD = 128
ATT_HEADS = 8
ATT_WIDTH = ATT_HEADS * ATT_HD
DIL_CONFIGS = ((128, 1), (512, 4), (2048, 16))
ATT_BLOCK = 128
ROPE_THETA = 10000.0

VMEM_LIMIT_BYTES = 56 * 1024 * 1024
LANES = 128

COL_QG, COL_KG, COL_VG, COL_RG = 0, 512, 1024, 2048
COL_QA, COL_KA, COL_VA = 3072, 4096, 5120
D_MAIN = 6144
PROJ_TN = 512


def _params(sem):
    return pltpu.CompilerParams(dimension_semantics=sem, vmem_limit_bytes=VMEM_LIMIT_BYTES)


def _rms_scale(x):
    return lax.rsqrt(jnp.mean(x * x, axis=-1, keepdims=True) + EPS)


def _dot(a, b):
    return jnp.dot(a, b, preferred_element_type=F32)


def _dot_nt(a, b):
    return lax.dot_general(a, b, (((1,), (1,)), ((), ())), preferred_element_type=F32)


def _dot_tn(a, b):
    return lax.dot_general(a, b, (((0,), (0,)), ((), ())), preferred_element_type=F32)


FFN_TM = 512
FFN_TF = 512


def _ffn_body(x_ref, gain_ref, wg_ref, wu_ref, wd_ref, o_ref, h_ref):
    j = pl.program_id(1)

    @pl.when(j == 0)
    def _():
        x = x_ref[...]
        h_ref[...] = (x * _rms_scale(x) * gain_ref[...]).astype(BF16)
        o_ref[...] = x

    h = h_ref[...]
    g = _dot(h, wg_ref[...])
    u = _dot(h, wu_ref[...])
    a = (g * jax.nn.sigmoid(g) * u * FFN_RESIDUAL_WEIGHT).astype(BF16)
    o_ref[...] += _dot(a, wd_ref[...])


def _ffn(x2d, gain, wg, wu, wd):
    t = x2d.shape[0]
    return pl.pallas_call(
        _ffn_body,
        grid=(t // FFN_TM, D_FF // FFN_TF),
        in_specs=[
            pl.BlockSpec((FFN_TM, D_MODEL), lambda i, j: (i, 0)),
            pl.BlockSpec((1, D_MODEL), lambda i, j: (0, 0)),
            pl.BlockSpec((D_MODEL, FFN_TF), lambda i, j: (0, j)),
            pl.BlockSpec((D_MODEL, FFN_TF), lambda i, j: (0, j)),
            pl.BlockSpec((FFN_TF, D_MODEL), lambda i, j: (j, 0)),
        ],
        out_specs=pl.BlockSpec((FFN_TM, D_MODEL), lambda i, j: (i, 0)),
        out_shape=jax.ShapeDtypeStruct((t, D_MODEL), F32),
        scratch_shapes=[pltpu.VMEM((FFN_TM, D_MODEL), BF16)],
        compiler_params=_params(("parallel", "arbitrary")),
        name="ffn",
    )(x2d, gain, wg, wu, wd)


PROJ_TM = 512


def _log_sigmoid(x):
    return jnp.minimum(x, 0.0) - jnp.log1p(jnp.exp(-jnp.abs(x)))


def _rope_heads(z, gain, cos, sin):
    outs = []
    for hd in range(z.shape[1] // ATT_HD):
        zh = z[:, hd * ATT_HD:(hd + 1) * ATT_HD]
        zh = zh * _rms_scale(zh) * gain
        outs.append(zh * cos + pltpu.roll(zh, ATT_HD // 2, 1) * sin)
    return jnp.concatenate(outs, axis=1)


def _proj_body(x_ref, gain_ref, w_ref, wlow_ref, gup_ref, gbias_ref, qn_ref, kn_ref,
               cos_ref, sin_ref, z_ref, la_ref, h_ref):
    j = pl.program_id(1)

    @pl.when(j == 0)
    def _():
        x = x_ref[...]
        h = (x * _rms_scale(x) * gain_ref[...]).astype(BF16)
        h_ref[...] = h
        g_low = _dot(h, wlow_ref[...])
        logit = _dot(g_low.astype(BF16), gup_ref[...]) + gbias_ref[...]
        la_ref[...] = _log_sigmoid(logit) * (1.0 / GLA_GATE_TAU)

    z = _dot(h_ref[...], w_ref[...])
    q_lo, k_lo, v_lo = COL_QA // PROJ_TN, COL_KA // PROJ_TN, COL_VA // PROJ_TN

    @pl.when((j < q_lo) | (j >= v_lo))
    def _():
        z_ref[...] = z

    @pl.when((j >= q_lo) & (j < k_lo))
    def _():
        z_ref[...] = _rope_heads(z, qn_ref[...], cos_ref[...], sin_ref[...])

    @pl.when((j >= k_lo) & (j < v_lo))
    def _():
        z_ref[...] = _rope_heads(z, kn_ref[...], cos_ref[...], sin_ref[...])


def _in_proj(x2d, gain, w_main, w_low, gate_up, gate_bias, q_norm, k_norm, cos, sin, seq):
    t = x2d.shape[0]
    n_seq_tiles = seq // PROJ_TM
    const = lambda i, j: (0, 0)
    return pl.pallas_call(
        _proj_body,
        grid=(t // PROJ_TM, D_MAIN // PROJ_TN),
        in_specs=[
            pl.BlockSpec((PROJ_TM, D_MODEL), lambda i, j: (i, 0)),
            pl.BlockSpec((1, D_MODEL), const),
            pl.BlockSpec((D_MODEL, PROJ_TN), lambda i, j: (0, j)),
            pl.BlockSpec((D_MODEL, LANES), const),
            pl.BlockSpec((LANES, GLA_QK), const),
            pl.BlockSpec((1, GLA_QK), const),
            pl.BlockSpec((1, ATT_HD), const),
            pl.BlockSpec((1, ATT_HD), const),
            pl.BlockSpec((PROJ_TM, ATT_HD), lambda i, j: (i % n_seq_tiles, 0)),
            pl.BlockSpec((PROJ_TM, ATT_HD), lambda i, j: (i % n_seq_tiles, 0)),
        ],
        out_specs=[
            pl.BlockSpec((PROJ_TM, PROJ_TN), lambda i, j: (i, j)),
            pl.BlockSpec((PROJ_TM, GLA_QK), lambda i, j: (i, 0)),
        ],
        out_shape=[
            jax.ShapeDtypeStruct((t, D_MAIN), F32),
            jax.ShapeDtypeStruct((t, GLA_QK), F32),
        ],
        scratch_shapes=[pltpu.VMEM((PROJ_TM, D_MODEL), BF16)],
        compiler_params=_params(("parallel", "arbitrary")),
        name="in_proj",
    )(x2d, gain, w_main, w_low, gate_up, gate_bias, q_norm, k_norm, cos, sin)


GLA_LS = 1024
GLA_GROUP = 256


def _gla_consts():
    i = np.arange(GLA_GROUP)
    same = (i[:, None] // GLA_CHUNK) == (i[None, :] // GLA_CHUNK)
    tri = same & (i[None, :] <= i[:, None])
    ref = same & ((i[None, :] % GLA_CHUNK) <= GLA_CHUNK // 2 - 1)
    csum = np.concatenate([tri, ref, same], axis=0).astype(np.float32)
    return jnp.asarray(csum, BF16), jnp.asarray(tri.astype(np.float32))


def _gla_body(q_ref, k_ref, v_ref, r_ref, la_ref, gain_ref, csum_ref, tri_ref, o_ref, st_ref):
    @pl.when(pl.program_id(2) == 0)
    def _():
        st_ref[...] = jnp.zeros_like(st_ref)

    scale = GLA_DK ** -0.5
    csum = csum_ref[...]
    tri = tri_ref[...] > 0.5
    gain = gain_ref[...]
    for g in range(GLA_LS // GLA_GROUP):
        rows = slice(g * GLA_GROUP, (g + 1) * GLA_GROUP)
        la = la_ref[0, rows, :]
        la_hi = la.astype(BF16)
        la_lo = (la - la_hi.astype(F32)).astype(BF16)
        cs = _dot(csum, la_hi) + _dot(csum, la_lo)
        b = cs[:GLA_GROUP]
        b_ref = cs[GLA_GROUP:2 * GLA_GROUP]
        b_last = cs[2 * GLA_GROUP:]

        q = q_ref[0, rows, :] * scale
        k = k_ref[0, rows, :]
        v = v_ref[0, rows, :].astype(BF16)
        q_intra = (q * jnp.exp(b - b_ref)).astype(BF16)
        k_intra = (k * jnp.exp(b_ref - b)).astype(BF16)
        scores = jnp.where(tri, _dot_nt(q_intra, k_intra), 0.0)
        o = _dot(scores.astype(BF16), v)

        q_inter = (q * jnp.exp(b)).astype(BF16)
        k_state = (k * jnp.exp(b_last - b)).astype(BF16)
        o_inter = []
        for c in range(GLA_GROUP // GLA_CHUNK):
            cr = slice(c * GLA_CHUNK, (c + 1) * GLA_CHUNK)
            st = st_ref[...]
            o_inter.append(_dot_nt(q_inter[cr], st.astype(BF16)))
            decay = jnp.exp(b_last[c * GLA_CHUNK:c * GLA_CHUNK + 1, :])
            st_ref[...] = st * decay + _dot_tn(v[cr], k_state[cr])
        o = o + jnp.concatenate(o_inter, axis=0)

        o = o * _rms_scale(o) * gain
        r = r_ref[0, rows, :]
        o_ref[0, rows, :] = (o * (r * jax.nn.sigmoid(r))).astype(o_ref.dtype)


def _gla(z3, la3, gain):
    bsz, seq, _ = z3.shape
    csum, tri = _gla_consts()
    const = lambda b, h, s: (0, 0)
    return pl.pallas_call(
        _gla_body,
        grid=(bsz, GLA_HEADS, seq // GLA_LS),
        in_specs=[
            pl.BlockSpec((1, GLA_LS, GLA_DK), lambda b, h, s: (b, s, COL_QG // GLA_DK + h)),
            pl.BlockSpec((1, GLA_LS, GLA_DK), lambda b, h, s: (b, s, COL_KG // GLA_DK + h)),
            pl.BlockSpec((1, GLA_LS, GLA_DV), lambda b, h, s: (b, s, COL_VG // GLA_DV + h)),
            pl.BlockSpec((1, GLA_LS, GLA_DV), lambda b, h, s: (b, s, COL_RG // GLA_DV + h)),
            pl.BlockSpec((1, GLA_LS, GLA_DK), lambda b, h, s: (b, s, h)),
            pl.BlockSpec((1, GLA_DV), const),
            pl.BlockSpec((3 * GLA_GROUP, GLA_GROUP), const),
            pl.BlockSpec((GLA_GROUP, GLA_GROUP), const),
        ],
        out_specs=pl.BlockSpec((1, GLA_LS, GLA_DV), lambda b, h, s: (b, s, h)),
        out_shape=jax.ShapeDtypeStruct((bsz, seq, GLA_WIDTH), BF16),
        scratch_shapes=[pltpu.VMEM((GLA_DV, GLA_DK), F32)],
        compiler_params=_params(("parallel", "parallel", "arbitrary")),
        name="gla",
    )(z3, z3, z3, z3, la3, gain, csum, tri)


NEG = -1e30
ATT_MERGE_ROWS = 512


def _att_consts():
    qi = np.arange(ATT_BLOCK)[:, None]
    kj = np.arange(2 * ATT_BLOCK)[None, :]
    first = kj <= qi
    dist = qi + ATT_BLOCK - kj
    band = (dist >= 0) & (dist <= ATT_BLOCK)
    bias = np.where(np.stack([first, band]), 0.0, NEG).astype(np.float32)
    return jnp.asarray(bias), jnp.ones((2 * ATT_BLOCK, LANES), BF16)


def _att_body(q_ref, k_ref, v_ref, bias_ref, ones_ref, o_ref, ob_ref, lse_ref, *, seq):
    scale = ATT_HD ** -0.5
    ones = ones_ref[...]

    for br, (window, r) in enumerate(DIL_CONFIGS):
        assert window // r == ATT_BLOCK
        n_blocks = seq // (ATT_BLOCK * r)
        assert n_blocks >= 2
        span = ATT_BLOCK * r

        def rows(start, size, r=r):
            return pl.ds(start, size) if r == 1 else pl.ds(start, size, stride=r)

        def block(idx, carry, br=br, r=r, span=span, rows=rows):
            n = idx // r
            c = idx % r
            q_start = c + n * span
            k_start = c + jnp.maximum(n - 1, 0) * span
            q = (q_ref[0, rows(q_start, ATT_BLOCK), :] * scale).astype(BF16)
            kk = k_ref[0, rows(k_start, 2 * ATT_BLOCK), :].astype(BF16)
            vv = v_ref[0, rows(k_start, 2 * ATT_BLOCK), :].astype(BF16)
            s = _dot_nt(q, kk) + bias_ref[jnp.minimum(n, 1)]
            m = jnp.max(s, axis=-1, keepdims=True)
            p = jnp.exp(s - m).astype(BF16)
            l = _dot(p, ones)
            o = _dot(p, vv)
            ob_ref[br, rows(q_start, ATT_BLOCK), :] = o / l
            lse_ref[br, rows(q_start, ATT_BLOCK), :] = m + jnp.log(l)
            return carry

        lax.fori_loop(0, n_blocks * r, block, 0)

    def merge(i, carry):
        rs = pl.ds(pl.multiple_of(i * ATT_MERGE_ROWS, ATT_MERGE_ROWS), ATT_MERGE_ROWS)
        lses = [lse_ref[b, rs, :] for b in range(len(DIL_CONFIGS))]
        m = functools.reduce(jnp.maximum, lses)
        ws = [jnp.exp(x - m) for x in lses]
        den = functools.reduce(lambda a, b: a + b, ws)
        num = functools.reduce(lambda a, b: a + b, [w * ob_ref[b, rs, :] for b, w in enumerate(ws)])
        o_ref[0, rs, :] = (num / den).astype(o_ref.dtype)
        return carry

    lax.fori_loop(0, seq // ATT_MERGE_ROWS, merge, 0)


def _dil_attn(z3):
    bsz, seq, _ = z3.shape
    bias, ones = _att_consts()
    nb = len(DIL_CONFIGS)
    return pl.pallas_call(
        functools.partial(_att_body, seq=seq),
        grid=(bsz, ATT_HEADS),
        in_specs=[
            pl.BlockSpec((1, seq, ATT_HD), lambda b, h: (b, 0, COL_QA // ATT_HD + h)),
            pl.BlockSpec((1, seq, ATT_HD), lambda b, h: (b, 0, COL_KA // ATT_HD + h)),
            pl.BlockSpec((1, seq, ATT_HD), lambda b, h: (b, 0, COL_VA // ATT_HD + h)),
            pl.BlockSpec((2, ATT_BLOCK, 2 * ATT_BLOCK), lambda b, h: (0, 0, 0)),
            pl.BlockSpec((2 * ATT_BLOCK, LANES), lambda b, h: (0, 0)),
        ],
        out_specs=pl.BlockSpec((1, seq, ATT_HD), lambda b, h: (b, 0, h)),
        out_shape=jax.ShapeDtypeStruct((bsz, seq, ATT_WIDTH), BF16),
        scratch_shapes=[pltpu.VMEM((nb, seq, ATT_HD), F32), pltpu.VMEM((nb, seq, ATT_HD), F32)],
        compiler_params=_params(("parallel", "parallel")),
        name="dil_attn",
    )(z3, z3, z3, bias, ones)


OUT_TM = 1024
OUT_TN = 512


def _out_body(x_ref, og_ref, oa_ref, wg_ref, wa_ref, o_ref):
    o_ref[...] = x_ref[...] + _dot(og_ref[...], wg_ref[...]) + _dot(oa_ref[...], wa_ref[...])


def _out_proj(x2d, og, oa, w_top, w_bot):
    t = x2d.shape[0]
    return pl.pallas_call(
        _out_body,
        grid=(t // OUT_TM, D_MODEL // OUT_TN),
        in_specs=[
            pl.BlockSpec((OUT_TM, OUT_TN), lambda i, j: (i, j)),
            pl.BlockSpec((OUT_TM, GLA_WIDTH), lambda i, j: (i, 0)),
            pl.BlockSpec((OUT_TM, ATT_WIDTH), lambda i, j: (i, 0)),
            pl.BlockSpec((GLA_WIDTH, OUT_TN), lambda i, j: (0, j)),
            pl.BlockSpec((ATT_WIDTH, OUT_TN), lambda i, j: (0, j)),
        ],
        out_specs=pl.BlockSpec((OUT_TM, OUT_TN), lambda i, j: (i, j)),
        out_shape=jax.ShapeDtypeStruct((t, D_MODEL), F32),
        compiler_params=_params(("parallel", "arbitrary")),
        name="out_proj",
    )(x2d, og, oa, w_top, w_bot)


def _rope_tables(seq):
    half = ATT_HD // 2
    inv_freq = 1.0 / (ROPE_THETA ** (jnp.arange(half, dtype=F32) / half))
    ang = jnp.arange(seq, dtype=F32)[:, None] * inv_freq[None, :]
    cos, sin = jnp.cos(ang), jnp.sin(ang)
    return jnp.concatenate([cos, cos], axis=1), jnp.concatenate([-sin, sin], axis=1)


def _split_w_in(w_in):
    sizes = (GLA_QK, GLA_QK, GLA_WIDTH, GLA_GATE_RANK, GLA_WIDTH, ATT_WIDTH, ATT_WIDTH, ATT_WIDTH)
    offs = np.concatenate([[0], np.cumsum(sizes)])
    seg = [w_in[:, offs[i]:offs[i + 1]] for i in range(len(sizes))]
    q_g, k_g, v_g, g_low, r_g, q_a, k_a, v_a = seg
    w_main = jnp.concatenate([q_g, k_g, v_g, r_g, q_a, k_a, v_a], axis=1).astype(BF16)
    w_low = jnp.pad(g_low, ((0, 0), (0, LANES - GLA_GATE_RANK))).astype(BF16)
    return w_main, w_low


def kernel(x, ffn1_norm, ffn1_w_gate, ffn1_w_up, ffn1_w_down, mix_norm, w_in, gla_gate_up, gla_gate_bias, gla_out_norm, att_q_norm, att_k_norm, w_out, ffn2_norm, ffn2_w_gate, ffn2_w_up, ffn2_w_down):
    bsz, seq, d = x.shape
    depth = ffn1_norm.shape[0]
    t = bsz * seq
    cos, sin = _rope_tables(seq)
    x2d = x.reshape(t, d)
    for l in range(depth):
        x2d = _ffn(x2d, ffn1_norm[l][None], ffn1_w_gate[l].astype(BF16), ffn1_w_up[l].astype(BF16),
                   ffn1_w_down[l].astype(BF16))

        w_main, w_low = _split_w_in(w_in[l])
        gate_up = jnp.pad(gla_gate_up[l], ((0, LANES - GLA_GATE_RANK), (0, 0))).astype(BF16)
        z, la = _in_proj(x2d, mix_norm[l][None], w_main, w_low, gate_up, gla_gate_bias[l][None],
                         att_q_norm[l][None], att_k_norm[l][None], cos, sin, seq)
        z3 = z.reshape(bsz, seq, D_MAIN)
        o_g = _gla(z3, la.reshape(bsz, seq, GLA_QK), gla_out_norm[l][None])
        o_a = _dil_attn(z3)

        w_o = w_out[l].astype(BF16)
        x2d = _out_proj(x2d, o_g.reshape(t, GLA_WIDTH), o_a.reshape(t, ATT_WIDTH),
                        w_o[:GLA_WIDTH], w_o[GLA_WIDTH:])

        x2d = _ffn(x2d, ffn2_norm[l][None], ffn2_w_gate[l].astype(BF16), ffn2_w_up[l].astype(BF16),
                   ffn2_w_down[l].astype(BF16))
    return x2d.reshape(bsz, seq, d)
```

```python
import functools

import numpy as np
import jax
import jax.numpy as jnp
from jax import lax
from jax.experimental import pallas as pl
from jax.experimental.pallas import tpu as pltpu

F32 = jnp.float32
BF16 = jnp.bfloat16

D_MODEL = 2048
D_FF = 5632
EPS = 1e-6
FFN_RESIDUAL_WEIGHT = 0.5

GLA_HEADS = 4
GLA_DK = 128
GLA_DV = 256
GLA_QK = GLA_HEADS * GLA_DK
GLA_WIDTH = GLA_HEADS * GLA_DV
GLA_GATE_RANK = 16
GLA_GATE_TAU = 16.0
GLA_CHUNK = 64

ATT_HD = 128
ATT_HEADS = 8
ATT_WIDTH = ATT_HEADS * ATT_HD
DIL_CONFIGS = ((128, 1), (512, 4), (2048, 16))
ATT_BLOCK = 128
ROPE_THETA = 10000.0

VMEM_LIMIT_BYTES = 56 * 1024 * 1024
LANES = 128

COL_QG, COL_KG, COL_VG, COL_RG = 0, 512, 1024, 2048
COL_QA, COL_KA, COL_VA = 3072, 4096, 5120
D_MAIN = 6144
PROJ_TN = 2048


def _params(sem):
    return pltpu.CompilerParams(dimension_semantics=sem, vmem_limit_bytes=VMEM_LIMIT_BYTES)


def _rms_scale(x):
    return lax.rsqrt(jnp.mean(x * x, axis=-1, keepdims=True) + EPS)


def _dot(a, b):
    return jnp.dot(a, b, preferred_element_type=F32)


def _dot_nt(a, b):
    return lax.dot_general(a, b, (((1,), (1,)), ((), ())), preferred_element_type=F32)


def _dot_tn(a, b):
    return lax.dot_general(a, b, (((0,), (0,)), ((), ())), preferred_element_type=F32)


FFN_TM = 512
FFN_TF = 512


def _ffn_body(x_ref, gain_ref, wg_ref, wu_ref, wd_ref, o_ref, h_ref):
    j = pl.program_id(1)

    @pl.when(j == 0)
    def _():
        x = x_ref[...]
        h_ref[...] = (x * _rms_scale(x) * gain_ref[...]).astype(BF16)
        o_ref[...] = x

    h = h_ref[...]
    g = _dot(h, wg_ref[...])
    u = _dot(h, wu_ref[...])
    a = (g * jax.nn.sigmoid(g) * u * FFN_RESIDUAL_WEIGHT).astype(BF16)
    o_ref[...] += _dot(a, wd_ref[...])


def _ffn(x2d, gain, wg, wu, wd):
    t = x2d.shape[0]
    return pl.pallas_call(
        _ffn_body,
        grid=(t // FFN_TM, D_FF // FFN_TF),
        in_specs=[
            pl.BlockSpec((FFN_TM, D_MODEL), lambda i, j: (i, 0)),
            pl.BlockSpec((1, D_MODEL), lambda i, j: (0, 0)),
            pl.BlockSpec((D_MODEL, FFN_TF), lambda i, j: (0, j)),
            pl.BlockSpec((D_MODEL, FFN_TF), lambda i, j: (0, j)),
            pl.BlockSpec((FFN_TF, D_MODEL), lambda i, j: (j, 0)),
        ],
        out_specs=pl.BlockSpec((FFN_TM, D_MODEL), lambda i, j: (i, 0)),
        out_shape=jax.ShapeDtypeStruct((t, D_MODEL), F32),
        scratch_shapes=[pltpu.VMEM((FFN_TM, D_MODEL), BF16)],
        compiler_params=_params(("parallel", "arbitrary")),
        name="ffn",
    )(x2d, gain, wg, wu, wd)


PROJ_TM = 512
PROJ_PIECE = 256
LOG2E = 1.4426950408889634
ATT_Q_SCALE = ATT_HD ** -0.5 * LOG2E


def _log_sigmoid(x):
    return jnp.minimum(x, 0.0) - jnp.log1p(jnp.exp(-jnp.abs(x)))


def _rope_heads(z, gain, cos, sin):
    outs = []
    for hd in range(z.shape[1] // ATT_HD):
        zh = z[:, hd * ATT_HD:(hd + 1) * ATT_HD]
        zh = zh * _rms_scale(zh) * gain
        outs.append(zh * cos + pltpu.roll(zh, ATT_HD // 2, 1) * sin)
    return jnp.concatenate(outs, axis=1)


def _proj_body(x_ref, gain_ref, w_ref, wlow_ref, gup_ref, gbias_ref, qn_ref, kn_ref,
               cos_ref, sin_ref, z_ref, la_ref, h_ref):
    j = pl.program_id(1)

    @pl.when(j == 0)
    def _():
        x = x_ref[...]
        h = (x * _rms_scale(x) * gain_ref[...]).astype(BF16)
        h_ref[...] = h
        g_low = _dot(h, wlow_ref[...])
        logit = _dot(g_low.astype(BF16), gup_ref[...]) + gbias_ref[...]
        la_ref[...] = _log_sigmoid(logit) * (1.0 / GLA_GATE_TAU)

    def piece(c0, gain):
        cols = slice(c0, c0 + PROJ_PIECE)
        z = _dot(h_ref[...], w_ref[:, cols])
        if gain is not None:
            z = _rope_heads(z, gain, cos_ref[...], sin_ref[...])
        z_ref[:, cols] = z

    def has_epilogue(col):
        return COL_QA <= col < COL_VA

    def piece_gain(col):
        if not has_epilogue(col):
            return None
        return qn_ref[...] * ATT_Q_SCALE if col < COL_KA else kn_ref[...]

    for step in range(D_MAIN // PROJ_TN):
        @pl.when(j == step)
        def _(step=step):
            starts = list(range(0, PROJ_TN, PROJ_PIECE))
            starts.sort(key=lambda c0: not has_epilogue(step * PROJ_TN + c0))
            for c0 in starts:
                piece(c0, piece_gain(step * PROJ_TN + c0))


def _in_proj(x2d, gain, w_main, w_low, gate_up, gate_bias, q_norm, k_norm, cos, sin, seq):
    t = x2d.shape[0]
    n_seq_tiles = seq // PROJ_TM
    const = lambda i, j: (0, 0)
    return pl.pallas_call(
        _proj_body,
        grid=(t // PROJ_TM, D_MAIN // PROJ_TN),
        in_specs=[
            pl.BlockSpec((PROJ_TM, D_MODEL), lambda i, j: (i, 0)),
            pl.BlockSpec((1, D_MODEL), const),
            pl.BlockSpec((D_MODEL, PROJ_TN), lambda i, j: (0, j)),
            pl.BlockSpec((D_MODEL, LANES), const),
            pl.BlockSpec((LANES, GLA_QK), const),
            pl.BlockSpec((1, GLA_QK), const),
            pl.BlockSpec((1, ATT_HD), const),
            pl.BlockSpec((1, ATT_HD), const),
            pl.BlockSpec((PROJ_TM, ATT_HD), lambda i, j: (i % n_seq_tiles, 0)),
            pl.BlockSpec((PROJ_TM, ATT_HD), lambda i, j: (i % n_seq_tiles, 0)),
        ],
        out_specs=[
            pl.BlockSpec((PROJ_TM, PROJ_TN), lambda i, j: (i, j)),
            pl.BlockSpec((PROJ_TM, GLA_QK), lambda i, j: (i, 0)),
        ],
        out_shape=[
            jax.ShapeDtypeStruct((t, D_MAIN), F32),
            jax.ShapeDtypeStruct((t, GLA_QK), F32),
        ],
        scratch_shapes=[pltpu.VMEM((PROJ_TM, D_MODEL), BF16)],
        compiler_params=_params(("parallel", "arbitrary")),
        name="in_proj",
    )(x2d, gain, w_main, w_low, gate_up, gate_bias, q_norm, k_norm, cos, sin)


GLA_LS = 512
GLA_GROUP = 256


def _gla_consts():
    i = np.arange(GLA_GROUP)
    same = (i[:, None] // GLA_CHUNK) == (i[None, :] // GLA_CHUNK)
    tri = same & (i[None, :] <= i[:, None])
    ref = same & ((i[None, :] % GLA_CHUNK) <= GLA_CHUNK // 2 - 1)
    csum = np.concatenate([tri, ref, same], axis=0).astype(np.float32)
    return jnp.asarray(csum, BF16), jnp.asarray(tri.astype(np.float32))


def _gla_body(q_ref, k_ref, v_ref, r_ref, la_ref, gain_ref, csum_ref, tri_ref, o_ref, st_ref):
    @pl.when(pl.program_id(1) == 0)
    def _():
        st_ref[...] = jnp.zeros_like(st_ref)

    heads = range(GLA_HEADS)
    chunks = [slice(c * GLA_CHUNK, (c + 1) * GLA_CHUNK) for c in range(GLA_GROUP // GLA_CHUNK)]

    def group(g, carry):
        rows = pl.ds(pl.multiple_of(g * GLA_GROUP, GLA_GROUP), GLA_GROUP)
        ck = [slice(h * GLA_DK, (h + 1) * GLA_DK) for h in heads]
        cv = [slice(h * GLA_DV, (h + 1) * GLA_DV) for h in heads]
        csum = csum_ref[...]
        tri = tri_ref[...] > 0.5

        cs = []
        for h in heads:
            la = la_ref[0, rows, ck[h]]
            la_hi = la.astype(BF16)
            la_lo = (la - la_hi.astype(F32)).astype(BF16)
            cs.append(_dot(csum, la_hi) + _dot(csum, la_lo))
        b = [x[:GLA_GROUP] for x in cs]
        b_ref = [x[GLA_GROUP:2 * GLA_GROUP] for x in cs]
        b_last = [x[2 * GLA_GROUP:] for x in cs]

        q = [q_ref[0, rows, ck[h]] * (GLA_DK ** -0.5) for h in heads]
        k = [k_ref[0, rows, ck[h]] for h in heads]
        v = [v_ref[0, rows, cv[h]].astype(BF16) for h in heads]
        q_intra = [(q[h] * jnp.exp(b[h] - b_ref[h])).astype(BF16) for h in heads]
        k_intra = [(k[h] * jnp.exp(b_ref[h] - b[h])).astype(BF16) for h in heads]
        q_inter = [(q[h] * jnp.exp(b[h])).astype(BF16) for h in heads]
        k_state = [(k[h] * jnp.exp(b_last[h] - b[h])).astype(BF16) for h in heads]

        scores = [jnp.where(tri, _dot_nt(q_intra[h], k_intra[h]), 0.0).astype(BF16) for h in heads]
        kv = [[_dot_tn(v[h][cr], k_state[h][cr]) for cr in chunks] for h in heads]
        o = [_dot(scores[h], v[h]) for h in heads]

        st = [st_ref[h] for h in heads]
        o_inter = [[] for _ in heads]
        for c, cr in enumerate(chunks):
            for h in heads:
                o_inter[h].append(_dot_nt(q_inter[h][cr], st[h].astype(BF16)))
                decay = jnp.exp(b_last[h][c * GLA_CHUNK:c * GLA_CHUNK + 1, :])
                st[h] = st[h] * decay + kv[h][c]

        gain = gain_ref[...]
        for h in heads:
            st_ref[h] = st[h]
            oh = o[h] + jnp.concatenate(o_inter[h], axis=0)
            oh = oh * _rms_scale(oh) * gain
            r = r_ref[0, rows, cv[h]]
            o_ref[0, rows, cv[h]] = (oh * (r * jax.nn.sigmoid(r))).astype(o_ref.dtype)
        return carry

    lax.fori_loop(0, GLA_LS // GLA_GROUP, group, 0)


def _gla(z3, la3, gain):
    bsz, seq, _ = z3.shape
    csum, tri = _gla_consts()
    const = lambda b, s: (0, 0)
    return pl.pallas_call(
        _gla_body,
        grid=(bsz, seq // GLA_LS),
        in_specs=[
            pl.BlockSpec((1, GLA_LS, GLA_QK), lambda b, s: (b, s, COL_QG // GLA_QK)),
            pl.BlockSpec((1, GLA_LS, GLA_QK), lambda b, s: (b, s, COL_KG // GLA_QK)),
            pl.BlockSpec((1, GLA_LS, GLA_WIDTH), lambda b, s: (b, s, COL_VG // GLA_WIDTH)),
            pl.BlockSpec((1, GLA_LS, GLA_WIDTH), lambda b, s: (b, s, COL_RG // GLA_WIDTH)),
            pl.BlockSpec((1, GLA_LS, GLA_QK), lambda b, s: (b, s, 0)),
            pl.BlockSpec((1, GLA_DV), const),
            pl.BlockSpec((3 * GLA_GROUP, GLA_GROUP), const),
            pl.BlockSpec((GLA_GROUP, GLA_GROUP), const),
        ],
        out_specs=pl.BlockSpec((1, GLA_LS, GLA_WIDTH), lambda b, s: (b, s, 0)),
        out_shape=jax.ShapeDtypeStruct((bsz, seq, GLA_WIDTH), BF16),
        scratch_shapes=[pltpu.VMEM((GLA_HEADS, GLA_DV, GLA_DK), F32)],
        compiler_params=_params(("parallel", "arbitrary")),
        name="gla",
    )(z3, z3, z3, z3, la3, gain, csum, tri)


NEG = -1e30
ATT_MERGE_ROWS = 512
ATT_UNROLL = 4


def _att_consts():
    qi = np.arange(ATT_BLOCK)[:, None]
    kj = np.arange(2 * ATT_BLOCK)[None, :]
    first = kj <= qi
    dist = qi + ATT_BLOCK - kj
    band = (dist >= 0) & (dist <= ATT_BLOCK)
    bias = np.where(np.stack([first, band]), 0.0, NEG).astype(np.float32)
    return jnp.asarray(bias), jnp.ones((2 * ATT_BLOCK, LANES), BF16)


def _att_body(q_ref, k_ref, v_ref, bias_ref, ones_ref, o_ref, ob_ref, lse_ref, *, seq):
    def blocks(br, r, specs):
        def rows(start, size):
            return pl.ds(start, size) if r == 1 else pl.ds(start, size, stride=r)

        ss, vs = [], []
        for q_start, k_start, n_keys, bias in specs:
            q = q_ref[0, rows(q_start, ATT_BLOCK), :].astype(BF16)
            kk = k_ref[0, rows(k_start, n_keys), :].astype(BF16)
            ss.append(_dot_nt(q, kk) + bias)
            vv = v_ref[0, rows(k_start, n_keys), :].astype(BF16)
            vs.append(jnp.concatenate([vv, ones_ref[:n_keys, :]], axis=1))
        ms = [jnp.max(s, axis=-1, keepdims=True) for s in ss]
        ps = [jnp.exp2(s - m).astype(BF16) for s, m in zip(ss, ms)]
        ols = [_dot(p, v) for p, v in zip(ps, vs)]
        for (q_start, _, _, _), m, ol in zip(specs, ms, ols):
            o, l = ol[:, :ATT_HD], ol[:, ATT_HD:]
            ob_ref[br, rows(q_start, ATT_BLOCK), :] = o / l
            lse_ref[br, rows(q_start, ATT_BLOCK), :] = m + jnp.log2(l)

    for br, (window, r) in enumerate(DIL_CONFIGS):
        assert window // r == ATT_BLOCK
        n_blocks = seq // (ATT_BLOCK * r)
        span = ATT_BLOCK * r
        assert n_blocks >= 2 and (n_blocks * r) % ATT_UNROLL == 0

        if r >= ATT_UNROLL:
            assert r % ATT_UNROLL == 0
            per_row = r // ATT_UNROLL

            def first_row(i, carry, br=br, r=r):
                cs = [i * ATT_UNROLL + u for u in range(ATT_UNROLL)]
                blocks(br, r, [(c, c, ATT_BLOCK, bias_ref[0, :, :ATT_BLOCK]) for c in cs])
                return carry

            def later_rows(i, carry, br=br, r=r, span=span, per_row=per_row):
                n = 1 + i // per_row
                cs = [(i % per_row) * ATT_UNROLL + u for u in range(ATT_UNROLL)]
                blocks(br, r, [(c + n * span, c + (n - 1) * span, 2 * ATT_BLOCK, bias_ref[1])
                               for c in cs])
                return carry

            lax.fori_loop(0, per_row, first_row, 0)
            lax.fori_loop(0, (n_blocks - 1) * per_row, later_rows, 0)
        else:
            assert r == 1
            def rows_iter(i, carry, br=br, r=r, span=span):
                specs = []
                for u in range(ATT_UNROLL):
                    n = i * ATT_UNROLL + u
                    if u == 0:
                        bias = bias_ref[jnp.minimum(n, 1)]
                        k_start = jnp.maximum(n - 1, 0) * span
                    else:
                        bias = bias_ref[1]
                        k_start = (n - 1) * span
                    specs.append((n * span, k_start, 2 * ATT_BLOCK, bias))
                blocks(br, r, specs)
                return carry

            lax.fori_loop(0, n_blocks // ATT_UNROLL, rows_iter, 0)

    def merge(i, carry):
        rs = pl.ds(pl.multiple_of(i * ATT_MERGE_ROWS, ATT_MERGE_ROWS), ATT_MERGE_ROWS)
        lses = [lse_ref[b, rs, :] for b in range(len(DIL_CONFIGS))]
        m = functools.reduce(jnp.maximum, lses)
        ws = [jnp.exp2(x - m) for x in lses]
        den = functools.reduce(lambda a, b: a + b, ws)
        num = functools.reduce(lambda a, b: a + b, [w * ob_ref[b, rs, :] for b, w in enumerate(ws)])
        o_ref[0, rs, :] = (num / den).astype(o_ref.dtype)
        return carry

    lax.fori_loop(0, seq // ATT_MERGE_ROWS, merge, 0)


def _dil_attn(z3):
    bsz, seq, _ = z3.shape
    bias, ones = _att_consts()
    nb = len(DIL_CONFIGS)
    return pl.pallas_call(
        functools.partial(_att_body, seq=seq),
        grid=(bsz, ATT_HEADS),
        in_specs=[
            pl.BlockSpec((1, seq, ATT_HD), lambda b, h: (b, 0, COL_QA // ATT_HD + h)),
            pl.BlockSpec((1, seq, ATT_HD), lambda b, h: (b, 0, COL_KA // ATT_HD + h)),
            pl.BlockSpec((1, seq, ATT_HD), lambda b, h: (b, 0, COL_VA // ATT_HD + h)),
            pl.BlockSpec((2, ATT_BLOCK, 2 * ATT_BLOCK), lambda b, h: (0, 0, 0)),
            pl.BlockSpec((2 * ATT_BLOCK, LANES), lambda b, h: (0, 0)),
        ],
        out_specs=pl.BlockSpec((1, seq, ATT_HD), lambda b, h: (b, 0, h)),
        out_shape=jax.ShapeDtypeStruct((bsz, seq, ATT_WIDTH), BF16),
        scratch_shapes=[pltpu.VMEM((nb, seq, ATT_HD), F32), pltpu.VMEM((nb, seq, ATT_HD), F32)],
        compiler_params=_params(("parallel", "parallel")),
        name="dil_attn",
    )(z3, z3, z3, bias, ones)


OUT_TM = 1024
OUT_TN = 512


def _out_body(x_ref, og_ref, oa_ref, wg_ref, wa_ref, o_ref):
    o_ref[...] = x_ref[...] + _dot(og_ref[...], wg_ref[...]) + _dot(oa_ref[...], wa_ref[...])


def _out_proj(x2d, og, oa, w_top, w_bot):
    t = x2d.shape[0]
    return pl.pallas_call(
        _out_body,
        grid=(t // OUT_TM, D_MODEL // OUT_TN),
        in_specs=[
            pl.BlockSpec((OUT_TM, OUT_TN), lambda i, j: (i, j)),
            pl.BlockSpec((OUT_TM, GLA_WIDTH), lambda i, j: (i, 0)),
            pl.BlockSpec((OUT_TM, ATT_WIDTH), lambda i, j: (i, 0)),
            pl.BlockSpec((GLA_WIDTH, OUT_TN), lambda i, j: (0, j)),
            pl.BlockSpec((ATT_WIDTH, OUT_TN), lambda i, j: (0, j)),
        ],
        out_specs=pl.BlockSpec((OUT_TM, OUT_TN), lambda i, j: (i, j)),
        out_shape=jax.ShapeDtypeStruct((t, D_MODEL), F32),
        compiler_params=_params(("parallel", "arbitrary")),
        name="out_proj",
    )(x2d, og, oa, w_top, w_bot)


def _rope_tables(seq):
    half = ATT_HD // 2
    inv_freq = 1.0 / (ROPE_THETA ** (jnp.arange(half, dtype=F32) / half))
    ang = jnp.arange(seq, dtype=F32)[:, None] * inv_freq[None, :]
    cos, sin = jnp.cos(ang), jnp.sin(ang)
    return jnp.concatenate([cos, cos], axis=1), jnp.concatenate([-sin, sin], axis=1)


def _split_w_in(w_in):
    sizes = (GLA_QK, GLA_QK, GLA_WIDTH, GLA_GATE_RANK, GLA_WIDTH, ATT_WIDTH, ATT_WIDTH, ATT_WIDTH)
    offs = np.concatenate([[0], np.cumsum(sizes)])
    seg = [w_in[:, offs[i]:offs[i + 1]] for i in range(len(sizes))]
    q_g, k_g, v_g, g_low, r_g, q_a, k_a, v_a = seg
    w_main = jnp.concatenate([q_g, k_g, v_g, r_g, q_a, k_a, v_a], axis=1).astype(BF16)
    w_low = jnp.pad(g_low, ((0, 0), (0, LANES - GLA_GATE_RANK))).astype(BF16)
    return w_main, w_low


def kernel(x, ffn1_norm, ffn1_w_gate, ffn1_w_up, ffn1_w_down, mix_norm, w_in, gla_gate_up, gla_gate_bias, gla_out_norm, att_q_norm, att_k_norm, w_out, ffn2_norm, ffn2_w_gate, ffn2_w_up, ffn2_w_down):
    bsz, seq, d = x.shape
    depth = ffn1_norm.shape[0]
    t = bsz * seq
    cos, sin = _rope_tables(seq)
    x2d = x.reshape(t, d)
    for l in range(depth):
        x2d = _ffn(x2d, ffn1_norm[l][None], ffn1_w_gate[l].astype(BF16), ffn1_w_up[l].astype(BF16),
                   ffn1_w_down[l].astype(BF16))

        w_main, w_low = _split_w_in(w_in[l])
        gate_up = jnp.pad(gla_gate_up[l], ((0, LANES - GLA_GATE_RANK), (0, 0))).astype(BF16)
        z, la = _in_proj(x2d, mix_norm[l][None], w_main, w_low, gate_up, gla_gate_bias[l][None],
                         att_q_norm[l][None], att_k_norm[l][None], cos, sin, seq)
        z3 = z.reshape(bsz, seq, D_MAIN)
        o_g = _gla(z3, la.reshape(bsz, seq, GLA_QK), gla_out_norm[l][None])
        o_a = _dil_attn(z3)

        w_o = w_out[l].astype(BF16)
        x2d = _out_proj(x2d, o_g.reshape(t, GLA_WIDTH), o_a.reshape(t, ATT_WIDTH),
                        w_o[:GLA_WIDTH], w_o[GLA_WIDTH:])

        x2d = _ffn(x2d, ffn2_norm[l][None], ffn2_w_gate[l].astype(BF16), ffn2_w_up[l].astype(BF16),
                   ffn2_w_down[l].astype(BF16))
    return x2d.reshape(bsz, seq, d)
```

```python
import functools

import numpy as np
import jax
import jax.numpy as jnp
from jax import lax
from jax.experimental import pallas as pl
from jax.experimental.pallas import tpu as pltpu

F32 = jnp.float32
BF16 = jnp.bfloat16

D_MODEL = 2048
D_FF = 5632
EPS = 1e-6
FFN_RESIDUAL_WEIGHT = 0.5

GLA_HEADS = 4
GLA_DK = 128
GLA_DV = 256
GLA_QK = GLA_HEADS * GLA_DK
GLA_WIDTH = GLA_HEADS * GLA_DV
GLA_GATE_RANK = 16
GLA_GATE_TAU = 16.0
GLA_CHUNK = 64

ATT_HD = 128
ATT_HEADS = 8
ATT_WIDTH = ATT_HEADS * ATT_HD
DIL_CONFIGS = ((128, 1), (512, 4), (2048, 16))
ATT_BLOCK = 128
ROPE_THETA = 10000.0

VMEM_LIMIT_BYTES = 56 * 1024 * 1024
LANES = 128

COL_QG, COL_KG, COL_VG, COL_RG = 0, 512, 1024, 2048
COL_QA, COL_KA, COL_VA = 3072, 4096, 5120
D_MAIN = 6144
PROJ_TN = 2048


def _params(sem):
    return pltpu.CompilerParams(dimension_semantics=sem, vmem_limit_bytes=VMEM_LIMIT_BYTES)


def _rms_scale(x):
    return lax.rsqrt(jnp.mean(x * x, axis=-1, keepdims=True) + EPS)


def _dot(a, b):
    return jnp.dot(a, b, preferred_element_type=F32)


def _dot_nt(a, b):
    return lax.dot_general(a, b, (((1,), (1,)), ((), ())), preferred_element_type=F32)


def _dot_tn(a, b):
    return lax.dot_general(a, b, (((0,), (0,)), ((), ())), preferred_element_type=F32)


FFN_TM = 512
FFN_TF = 512


def _ffn_body(x_ref, gain_ref, wg_ref, wu_ref, wd_ref, o_ref, h_ref):
    j = pl.program_id(1)

    @pl.when(j == 0)
    def _():
        x = x_ref[...]
        h_ref[...] = (x * _rms_scale(x) * gain_ref[...]).astype(BF16)
        o_ref[...] = x

    h = h_ref[...]
    g = _dot(h, wg_ref[...])
    u = _dot(h, wu_ref[...])
    a = (g * jax.nn.sigmoid(g) * u * FFN_RESIDUAL_WEIGHT).astype(BF16)
    o_ref[...] += _dot(a, wd_ref[...])


def _ffn(x2d, gain, wg, wu, wd):
    t = x2d.shape[0]
    return pl.pallas_call(
        _ffn_body,
        grid=(t // FFN_TM, D_FF // FFN_TF),
        in_specs=[
            pl.BlockSpec((FFN_TM, D_MODEL), lambda i, j: (i, 0)),
            pl.BlockSpec((1, D_MODEL), lambda i, j: (0, 0)),
            pl.BlockSpec((D_MODEL, FFN_TF), lambda i, j: (0, j)),
            pl.BlockSpec((D_MODEL, FFN_TF), lambda i, j: (0, j)),
            pl.BlockSpec((FFN_TF, D_MODEL), lambda i, j: (j, 0)),
        ],
        out_specs=pl.BlockSpec((FFN_TM, D_MODEL), lambda i, j: (i, 0)),
        out_shape=jax.ShapeDtypeStruct((t, D_MODEL), F32),
        scratch_shapes=[pltpu.VMEM((FFN_TM, D_MODEL), BF16)],
        compiler_params=_params(("parallel", "arbitrary")),
        name="ffn",
    )(x2d, gain, wg, wu, wd)


PROJ_TM = 512
PROJ_PIECE = 256
LOG2E = 1.4426950408889634
ATT_Q_SCALE = ATT_HD ** -0.5 * LOG2E


def _log_sigmoid(x):
    return jnp.minimum(x, 0.0) - jnp.log1p(jnp.exp(-jnp.abs(x)))


def _rope_heads(z, gain, cos, sin):
    outs = []
    for hd in range(z.shape[1] // ATT_HD):
        zh = z[:, hd * ATT_HD:(hd + 1) * ATT_HD]
        zh = zh * _rms_scale(zh) * gain
        outs.append(zh * cos + pltpu.roll(zh, ATT_HD // 2, 1) * sin)
    return jnp.concatenate(outs, axis=1)


def _proj_body(x_ref, gain_ref, w_ref, wlow_ref, gup_ref, gbias_ref, qn_ref, kn_ref,
               cos_ref, sin_ref, z_ref, la_ref, h_ref):
    j = pl.program_id(1)

    @pl.when(j == 0)
    def _():
        x = x_ref[...]
        h = (x * _rms_scale(x) * gain_ref[...]).astype(BF16)
        h_ref[...] = h
        g_low = _dot(h, wlow_ref[...])
        logit = _dot(g_low.astype(BF16), gup_ref[...]) + gbias_ref[...]
        la_ref[...] = _log_sigmoid(logit) * (1.0 / GLA_GATE_TAU)

    def piece(c0, gain):
        cols = slice(c0, c0 + PROJ_PIECE)
        z = _dot(h_ref[...], w_ref[:, cols])
        if gain is not None:
            z = _rope_heads(z, gain, cos_ref[...], sin_ref[...])
        z_ref[:, cols] = z.astype(z_ref.dtype)

    def has_epilogue(col):
        return COL_QA <= col < COL_VA

    def piece_gain(col):
        if not has_epilogue(col):
            return None
        return qn_ref[...] * ATT_Q_SCALE if col < COL_KA else kn_ref[...]

    for step in range(D_MAIN // PROJ_TN):
        @pl.when(j == step)
        def _(step=step):
            starts = list(range(0, PROJ_TN, PROJ_PIECE))
            starts.sort(key=lambda c0: not has_epilogue(step * PROJ_TN + c0))
            for c0 in starts:
                piece(c0, piece_gain(step * PROJ_TN + c0))


def _in_proj(x2d, gain, w_main, w_low, gate_up, gate_bias, q_norm, k_norm, cos, sin, seq):
    t = x2d.shape[0]
    n_seq_tiles = seq // PROJ_TM
    const = lambda i, j: (0, 0)
    return pl.pallas_call(
        _proj_body,
        grid=(t // PROJ_TM, D_MAIN // PROJ_TN),
        in_specs=[
            pl.BlockSpec((PROJ_TM, D_MODEL), lambda i, j: (i, 0)),
            pl.BlockSpec((1, D_MODEL), const),
            pl.BlockSpec((D_MODEL, PROJ_TN), lambda i, j: (0, j)),
            pl.BlockSpec((D_MODEL, LANES), const),
            pl.BlockSpec((LANES, GLA_QK), const),
            pl.BlockSpec((1, GLA_QK), const),
            pl.BlockSpec((1, ATT_HD), const),
            pl.BlockSpec((1, ATT_HD), const),
            pl.BlockSpec((PROJ_TM, ATT_HD), lambda i, j: (i % n_seq_tiles, 0)),
            pl.BlockSpec((PROJ_TM, ATT_HD), lambda i, j: (i % n_seq_tiles, 0)),
        ],
        out_specs=[
            pl.BlockSpec((PROJ_TM, PROJ_TN), lambda i, j: (i, j)),
            pl.BlockSpec((PROJ_TM, GLA_QK), lambda i, j: (i, 0)),
        ],
        out_shape=[
            jax.ShapeDtypeStruct((t, D_MAIN), BF16),
            jax.ShapeDtypeStruct((t, GLA_QK), F32),
        ],
        scratch_shapes=[pltpu.VMEM((PROJ_TM, D_MODEL), BF16)],
        compiler_params=_params(("parallel", "arbitrary")),
        name="in_proj",
    )(x2d, gain, w_main, w_low, gate_up, gate_bias, q_norm, k_norm, cos, sin)


GLA_LS = 512
GLA_GROUP = 256


def _gla_consts():
    i = np.arange(GLA_GROUP)
    same = (i[:, None] // GLA_CHUNK) == (i[None, :] // GLA_CHUNK)
    tri = (same & (i[None, :] <= i[:, None])).astype(np.float32)
    return jnp.asarray(tri, BF16), jnp.asarray(tri)


def _gla_body(q_ref, k_ref, v_ref, r_ref, la_ref, gain_ref, csum_ref, tri_ref, o_ref, st_ref):
    @pl.when(pl.program_id(1) == 0)
    def _():
        st_ref[...] = jnp.zeros_like(st_ref)

    heads = range(GLA_HEADS)
    chunks = [slice(c * GLA_CHUNK, (c + 1) * GLA_CHUNK) for c in range(GLA_GROUP // GLA_CHUNK)]

    def group(g, carry):
        rows = pl.ds(pl.multiple_of(g * GLA_GROUP, GLA_GROUP), GLA_GROUP)
        ck = [slice(h * GLA_DK, (h + 1) * GLA_DK) for h in heads]
        cv = [slice(h * GLA_DV, (h + 1) * GLA_DV) for h in heads]
        csum = csum_ref[...]
        tri = tri_ref[...] > 0.5

        b = []
        for h in heads:
            la = la_ref[0, rows, ck[h]]
            la_hi = la.astype(BF16)
            la_lo = (la - la_hi.astype(F32)).astype(BF16)
            b.append(_dot(csum, la_hi) + _dot(csum, la_lo))

        def chunk_row(x, row):
            return jnp.concatenate(
                [jnp.broadcast_to(x[cr.start + row:cr.start + row + 1, :], (GLA_CHUNK, x.shape[1]))
                 for cr in chunks], axis=0)

        b_ref = [chunk_row(b[h], GLA_CHUNK // 2 - 1) for h in heads]
        b_last = [chunk_row(b[h], GLA_CHUNK - 1) for h in heads]

        q = [q_ref[0, rows, ck[h]].astype(F32) * (GLA_DK ** -0.5) for h in heads]
        k = [k_ref[0, rows, ck[h]].astype(F32) for h in heads]
        v = [v_ref[0, rows, cv[h]].astype(BF16) for h in heads]
        q_intra = [(q[h] * jnp.exp(b[h] - b_ref[h])).astype(BF16) for h in heads]
        k_intra = [(k[h] * jnp.exp(b_ref[h] - b[h])).astype(BF16) for h in heads]
        q_inter = [(q[h] * jnp.exp(b[h])).astype(BF16) for h in heads]
        k_state = [(k[h] * jnp.exp(b_last[h] - b[h])).astype(BF16) for h in heads]

        scores = [jnp.where(tri, _dot_nt(q_intra[h], k_intra[h]), 0.0).astype(BF16) for h in heads]
        kv = [[_dot_tn(v[h][cr], k_state[h][cr]) for cr in chunks] for h in heads]
        o = [_dot(scores[h], v[h]) for h in heads]

        st = [st_ref[h] for h in heads]
        o_inter = [[] for _ in heads]
        for c, cr in enumerate(chunks):
            for h in heads:
                o_inter[h].append(_dot_nt(q_inter[h][cr], st[h].astype(BF16)))
                decay = jnp.exp(b_last[h][c * GLA_CHUNK:c * GLA_CHUNK + 1, :])
                st[h] = st[h] * decay + kv[h][c]

        gain = gain_ref[...]
        for h in heads:
            st_ref[h] = st[h]
            oh = o[h] + jnp.concatenate(o_inter[h], axis=0)
            oh = oh * _rms_scale(oh) * gain
            r = r_ref[0, rows, cv[h]].astype(F32)
            o_ref[0, rows, cv[h]] = (oh * (r * jax.nn.sigmoid(r))).astype(o_ref.dtype)
        return carry

    lax.fori_loop(0, GLA_LS // GLA_GROUP, group, 0)


def _gla(z3, la3, gain):
    bsz, seq, _ = z3.shape
    csum, tri = _gla_consts()
    const = lambda b, s: (0, 0)
    return pl.pallas_call(
        _gla_body,
        grid=(bsz, seq // GLA_LS),
        in_specs=[
            pl.BlockSpec((1, GLA_LS, GLA_QK), lambda b, s: (b, s, COL_QG // GLA_QK)),
            pl.BlockSpec((1, GLA_LS, GLA_QK), lambda b, s: (b, s, COL_KG // GLA_QK)),
            pl.BlockSpec((1, GLA_LS, GLA_WIDTH), lambda b, s: (b, s, COL_VG // GLA_WIDTH)),
            pl.BlockSpec((1, GLA_LS, GLA_WIDTH), lambda b, s: (b, s, COL_RG // GLA_WIDTH)),
            pl.BlockSpec((1, GLA_LS, GLA_QK), lambda b, s: (b, s, 0)),
            pl.BlockSpec((1, GLA_DV), const),
            pl.BlockSpec((GLA_GROUP, GLA_GROUP), const),
            pl.BlockSpec((GLA_GROUP, GLA_GROUP), const),
        ],
        out_specs=pl.BlockSpec((1, GLA_LS, GLA_WIDTH), lambda b, s: (b, s, 0)),
        out_shape=jax.ShapeDtypeStruct((bsz, seq, GLA_WIDTH), BF16),
        scratch_shapes=[pltpu.VMEM((GLA_HEADS, GLA_DV, GLA_DK), F32)],
        compiler_params=_params(("parallel", "arbitrary")),
        name="gla",
    )(z3, z3, z3, z3, la3, gain, csum, tri)


NEG = -1e30
ATT_MERGE_ROWS = 512
ATT_UNROLL = 8


def _att_consts():
    qi = np.arange(ATT_BLOCK)[:, None]
    kj = np.arange(2 * ATT_BLOCK)[None, :]
    first = kj <= qi
    dist = qi + ATT_BLOCK - kj
    band = (dist >= 0) & (dist <= ATT_BLOCK)
    bias = np.where(np.stack([first, band]), 0.0, NEG).astype(np.float32)
    return jnp.asarray(bias), jnp.ones((2 * ATT_BLOCK, LANES), BF16)


def _att_body(qb_ref, kb_ref, vb_ref, bias_ref, ones_ref, o_ref, qkv_ref, ob_ref, lse_ref, *, seq):
    def widen(i, carry):
        rs = pl.ds(pl.multiple_of(i * ATT_MERGE_ROWS, ATT_MERGE_ROWS), ATT_MERGE_ROWS)
        for a, src in enumerate((qb_ref, kb_ref, vb_ref)):
            qkv_ref[a, rs, :] = src[0, rs, :].astype(F32)
        return carry

    lax.fori_loop(0, seq // ATT_MERGE_ROWS, widen, 0)

    def blocks(br, r, specs):
        def rows(start, size):
            return pl.ds(start, size) if r == 1 else pl.ds(start, size, stride=r)

        ss, vs = [], []
        for q_start, k_start, n_keys, bias in specs:
            q = qkv_ref[0, rows(q_start, ATT_BLOCK), :].astype(BF16)
            kk = qkv_ref[1, rows(k_start, n_keys), :].astype(BF16)
            ss.append(_dot_nt(q, kk) + bias)
            vv = qkv_ref[2, rows(k_start, n_keys), :].astype(BF16)
            vs.append(jnp.concatenate([vv, ones_ref[:n_keys, :]], axis=1))
        ms = [jnp.max(s, axis=-1, keepdims=True) for s in ss]
        ps = [jnp.exp2(s - m).astype(BF16) for s, m in zip(ss, ms)]
        ols = [_dot(p, v) for p, v in zip(ps, vs)]
        for (q_start, _, _, _), m, ol in zip(specs, ms, ols):
            o, l = ol[:, :ATT_HD], ol[:, ATT_HD:]
            ob_ref[br, rows(q_start, ATT_BLOCK), :] = o / l
            lse_ref[br, rows(q_start, ATT_BLOCK), :] = m + jnp.log2(l)

    for br, (window, r) in enumerate(DIL_CONFIGS):
        assert window // r == ATT_BLOCK
        n_blocks = seq // (ATT_BLOCK * r)
        span = ATT_BLOCK * r
        u_c = min(r, ATT_UNROLL)
        u_n = ATT_UNROLL // u_c
        assert n_blocks >= 2 and r % u_c == 0 and n_blocks % u_n == 0
        c_groups = r // u_c

        if u_n == 1:
            def first_row(i, carry, br=br, r=r, u_c=u_c):
                cs = [i * u_c + u for u in range(u_c)]
                blocks(br, r, [(c, c, ATT_BLOCK, bias_ref[0, :, :ATT_BLOCK]) for c in cs])
                return carry

            def later_rows(i, carry, br=br, r=r, span=span, u_c=u_c, c_groups=c_groups):
                n = 1 + i // c_groups
                cs = [(i % c_groups) * u_c + u for u in range(u_c)]
                blocks(br, r, [(c + n * span, c + (n - 1) * span, 2 * ATT_BLOCK, bias_ref[1])
                               for c in cs])
                return carry

            lax.fori_loop(0, c_groups, first_row, 0)
            lax.fori_loop(0, (n_blocks - 1) * c_groups, later_rows, 0)
        else:
            assert c_groups == 1
            def rows_iter(i, carry, br=br, r=r, span=span, u_n=u_n):
                specs = []
                for un in range(u_n):
                    n = i * u_n + un
                    if un == 0:
                        bias = bias_ref[jnp.minimum(n, 1)]
                        k_row = jnp.maximum(n - 1, 0) * span
                    else:
                        bias = bias_ref[1]
                        k_row = (n - 1) * span
                    specs += [(c + n * span, c + k_row, 2 * ATT_BLOCK, bias) for c in range(r)]
                blocks(br, r, specs)
                return carry

            lax.fori_loop(0, n_blocks // u_n, rows_iter, 0)

    def merge(i, carry):
        rs = pl.ds(pl.multiple_of(i * ATT_MERGE_ROWS, ATT_MERGE_ROWS), ATT_MERGE_ROWS)
        lses = [lse_ref[b, rs, :] for b in range(len(DIL_CONFIGS))]
        m = functools.reduce(jnp.maximum, lses)
        ws = [jnp.exp2(x - m) for x in lses]
        den = functools.reduce(lambda a, b: a + b, ws)
        num = functools.reduce(lambda a, b: a + b, [w * ob_ref[b, rs, :] for b, w in enumerate(ws)])
        o_ref[0, rs, :] = (num / den).astype(o_ref.dtype)
        return carry

    lax.fori_loop(0, seq // ATT_MERGE_ROWS, merge, 0)


def _dil_attn(z3):
    bsz, seq, _ = z3.shape
    bias, ones = _att_consts()
    nb = len(DIL_CONFIGS)
    return pl.pallas_call(
        functools.partial(_att_body, seq=seq),
        grid=(bsz, ATT_HEADS),
        in_specs=[
            pl.BlockSpec((1, seq, ATT_HD), lambda b, h: (b, 0, COL_QA // ATT_HD + h)),
            pl.BlockSpec((1, seq, ATT_HD), lambda b, h: (b, 0, COL_KA // ATT_HD + h)),
            pl.BlockSpec((1, seq, ATT_HD), lambda b, h: (b, 0, COL_VA // ATT_HD + h)),
            pl.BlockSpec((2, ATT_BLOCK, 2 * ATT_BLOCK), lambda b, h: (0, 0, 0)),
            pl.BlockSpec((2 * ATT_BLOCK, LANES), lambda b, h: (0, 0)),
        ],
        out_specs=pl.BlockSpec((1, seq, ATT_HD), lambda b, h: (b, 0, h)),
        out_shape=jax.ShapeDtypeStruct((bsz, seq, ATT_WIDTH), BF16),
        scratch_shapes=[pltpu.VMEM((3, seq, ATT_HD), F32),
                        pltpu.VMEM((nb, seq, ATT_HD), F32),
                        pltpu.VMEM((nb, seq, ATT_HD), F32)],
        compiler_params=_params(("parallel", "parallel")),
        name="dil_attn",
    )(z3, z3, z3, bias, ones)


OUT_TM = 1024
OUT_TN = 512


def _out_body(x_ref, og_ref, oa_ref, wg_ref, wa_ref, o_ref):
    o_ref[...] = x_ref[...] + _dot(og_ref[...], wg_ref[...]) + _dot(oa_ref[...], wa_ref[...])


def _out_proj(x2d, og, oa, w_top, w_bot):
    t = x2d.shape[0]
    return pl.pallas_call(
        _out_body,
        grid=(t // OUT_TM, D_MODEL // OUT_TN),
        in_specs=[
            pl.BlockSpec((OUT_TM, OUT_TN), lambda i, j: (i, j)),
            pl.BlockSpec((OUT_TM, GLA_WIDTH), lambda i, j: (i, 0)),
            pl.BlockSpec((OUT_TM, ATT_WIDTH), lambda i, j: (i, 0)),
            pl.BlockSpec((GLA_WIDTH, OUT_TN), lambda i, j: (0, j)),
            pl.BlockSpec((ATT_WIDTH, OUT_TN), lambda i, j: (0, j)),
        ],
        out_specs=pl.BlockSpec((OUT_TM, OUT_TN), lambda i, j: (i, j)),
        out_shape=jax.ShapeDtypeStruct((t, D_MODEL), F32),
        compiler_params=_params(("parallel", "arbitrary")),
        name="out_proj",
    )(x2d, og, oa, w_top, w_bot)


def _rope_tables(seq):
    half = ATT_HD // 2
    inv_freq = 1.0 / (ROPE_THETA ** (jnp.arange(half, dtype=F32) / half))
    ang = jnp.arange(seq, dtype=F32)[:, None] * inv_freq[None, :]
    cos, sin = jnp.cos(ang), jnp.sin(ang)
    return jnp.concatenate([cos, cos], axis=1), jnp.concatenate([-sin, sin], axis=1)


def _split_w_in(w_in):
    sizes = (GLA_QK, GLA_QK, GLA_WIDTH, GLA_GATE_RANK, GLA_WIDTH, ATT_WIDTH, ATT_WIDTH, ATT_WIDTH)
    offs = np.concatenate([[0], np.cumsum(sizes)])
    w_in = w_in.astype(BF16)
    seg = [w_in[:, offs[i]:offs[i + 1]] for i in range(len(sizes))]
    q_g, k_g, v_g, g_low, r_g, q_a, k_a, v_a = seg
    w_main = jnp.concatenate([q_g, k_g, v_g, r_g, q_a, k_a, v_a], axis=1)
    w_low = jnp.pad(g_low, ((0, 0), (0, LANES - GLA_GATE_RANK)))
    return w_main, w_low


def kernel(x, ffn1_norm, ffn1_w_gate, ffn1_w_up, ffn1_w_down, mix_norm, w_in, gla_gate_up, gla_gate_bias, gla_out_norm, att_q_norm, att_k_norm, w_out, ffn2_norm, ffn2_w_gate, ffn2_w_up, ffn2_w_down):
    bsz, seq, d = x.shape
    depth = ffn1_norm.shape[0]
    t = bsz * seq
    cos, sin = _rope_tables(seq)
    x2d = x.reshape(t, d)
    for l in range(depth):
        x2d = _ffn(x2d, ffn1_norm[l][None], ffn1_w_gate[l].astype(BF16), ffn1_w_up[l].astype(BF16),
                   ffn1_w_down[l].astype(BF16))

        w_main, w_low = _split_w_in(w_in[l])
        gate_up = jnp.pad(gla_gate_up[l], ((0, LANES - GLA_GATE_RANK), (0, 0))).astype(BF16)
        z, la = _in_proj(x2d, mix_norm[l][None], w_main, w_low, gate_up, gla_gate_bias[l][None],
                         att_q_norm[l][None], att_k_norm[l][None], cos, sin, seq)
        z3 = z.reshape(bsz, seq, D_MAIN)
        o_g = _gla(z3, la.reshape(bsz, seq, GLA_QK), gla_out_norm[l][None])
        o_a = _dil_attn(z3)

        w_o = w_out[l].astype(BF16)
        x2d = _out_proj(x2d, o_g.reshape(t, GLA_WIDTH), o_a.reshape(t, ATT_WIDTH),
                        w_o[:GLA_WIDTH], w_o[GLA_WIDTH:])

        x2d = _ffn(x2d, ffn2_norm[l][None], ffn2_w_gate[l].astype(BF16), ffn2_w_up[l].astype(BF16),
                   ffn2_w_down[l].astype(BF16))
    return x2d.reshape(bsz, seq, d)
```

```python
import functools

import numpy as np
import jax
import jax.numpy as jnp
from jax import lax
from jax.experimental import pallas as pl
from jax.experimental.pallas import tpu as pltpu

F32 = jnp.float32
BF16 = jnp.bfloat16

D_MODEL = 2048
D_FF = 5632
EPS = 1e-6
FFN_RESIDUAL_WEIGHT = 0.5

GLA_HEADS = 4
GLA_DK = 128
GLA_DV = 256
GLA_QK = GLA_HEADS * GLA_DK
GLA_WIDTH = GLA_HEADS * GLA_DV
GLA_GATE_RANK = 16
GLA_GATE_TAU = 16.0
GLA_CHUNK = 64

ATT_HD = 128
ATT_HEADS = 8
ATT_WIDTH = ATT_HEADS * ATT_HD
DIL_CONFIGS = ((128, 1), (512, 4), (2048, 16))
ATT_BLOCK = 128
ROPE_THETA = 10000.0

VMEM_LIMIT_BYTES = 56 * 1024 * 1024
LANES = 128

COL_QG, COL_KG, COL_VG, COL_RG = 0, 512, 1024, 2048
COL_QA, COL_KA, COL_VA = 3072, 4096, 5120
D_MAIN = 6144
PROJ_TN = 2048


def _params(sem):
    return pltpu.CompilerParams(dimension_semantics=sem, vmem_limit_bytes=VMEM_LIMIT_BYTES)


def _rms_scale(x):
    return lax.rsqrt(jnp.mean(x * x, axis=-1, keepdims=True) + EPS)


def _dot(a, b):
    return jnp.dot(a, b, preferred_element_type=F32)


def _dot_nt(a, b):
    return lax.dot_general(a, b, (((1,), (1,)), ((), ())), preferred_element_type=F32)


def _dot_tn(a, b):
    return lax.dot_general(a, b, (((0,), (0,)), ((), ())), preferred_element_type=F32)


FFN_TM = 1024
FFN_TF = 512


def _ffn_body(x_ref, gain_ref, wg_ref, wu_ref, wd_ref, o_ref, h_ref):
    j = pl.program_id(1)

    @pl.when(j == 0)
    def _():
        x = x_ref[...]
        h_ref[...] = (x * _rms_scale(x) * gain_ref[...]).astype(BF16)
        o_ref[...] = x

    h = h_ref[...]
    g = _dot(h, wg_ref[...])
    u = _dot(h, wu_ref[...])
    a = (g * jax.nn.sigmoid(g) * u * FFN_RESIDUAL_WEIGHT).astype(BF16)
    o_ref[...] += _dot(a, wd_ref[...])


def _ffn(x2d, gain, wg, wu, wd):
    t = x2d.shape[0]
    return pl.pallas_call(
        _ffn_body,
        grid=(t // FFN_TM, D_FF // FFN_TF),
        in_specs=[
            pl.BlockSpec((FFN_TM, D_MODEL), lambda i, j: (i, 0)),
            pl.BlockSpec((1, D_MODEL), lambda i, j: (0, 0)),
            pl.BlockSpec((D_MODEL, FFN_TF), lambda i, j: (0, j)),
            pl.BlockSpec((D_MODEL, FFN_TF), lambda i, j: (0, j)),
            pl.BlockSpec((FFN_TF, D_MODEL), lambda i, j: (j, 0)),
        ],
        out_specs=pl.BlockSpec((FFN_TM, D_MODEL), lambda i, j: (i, 0)),
        out_shape=jax.ShapeDtypeStruct((t, D_MODEL), F32),
        scratch_shapes=[pltpu.VMEM((FFN_TM, D_MODEL), BF16)],
        compiler_params=_params(("parallel", "arbitrary")),
        name="ffn",
    )(x2d, gain, wg, wu, wd)


PROJ_TM = 512
PROJ_PIECE = 256
LOG2E = 1.4426950408889634
ATT_Q_SCALE = ATT_HD ** -0.5 * LOG2E


def _log_sigmoid(x):
    return jnp.minimum(x, 0.0) - jnp.log1p(jnp.exp(-jnp.abs(x)))


def _rope_heads(z, gain, cos, sin):
    outs = []
    for hd in range(z.shape[1] // ATT_HD):
        zh = z[:, hd * ATT_HD:(hd + 1) * ATT_HD]
        zh = zh * _rms_scale(zh) * gain
        outs.append(zh * cos + pltpu.roll(zh, ATT_HD // 2, 1) * sin)
    return jnp.concatenate(outs, axis=1)


def _proj_body(x_ref, gain_ref, w_ref, wlow_ref, gup_ref, gbias_ref, qn_ref, kn_ref,
               cos_ref, sin_ref, z_ref, la_ref, h_ref):
    j = pl.program_id(1)

    def prologue():
        x = x_ref[...]
        h = (x * _rms_scale(x) * gain_ref[...]).astype(BF16)
        h_ref[...] = h
        g_low = _dot(h, wlow_ref[...])
        logit = _dot(g_low.astype(BF16), gup_ref[...]) + gbias_ref[...]
        la_ref[...] = _log_sigmoid(logit) * (1.0 / GLA_GATE_TAU)

    def piece(c0, gain):
        cols = slice(c0, c0 + PROJ_PIECE)
        z = _dot(h_ref[...], w_ref[:, cols])
        if gain is not None:
            z = _rope_heads(z, gain, cos_ref[...], sin_ref[...])
        z_ref[:, cols] = z.astype(z_ref.dtype)

    def has_epilogue(col):
        return COL_QA <= col < COL_VA

    def piece_gain(col):
        if not has_epilogue(col):
            return None
        return qn_ref[...] * ATT_Q_SCALE if col < COL_KA else kn_ref[...]

    for step in range(D_MAIN // PROJ_TN):
        @pl.when(j == step)
        def _(step=step):
            if step == 0:
                prologue()
            starts = list(range(0, PROJ_TN, PROJ_PIECE))
            starts.sort(key=lambda c0: not has_epilogue(step * PROJ_TN + c0))
            for c0 in starts:
                piece(c0, piece_gain(step * PROJ_TN + c0))


def _in_proj(x2d, gain, w_main, w_low, gate_up, gate_bias, q_norm, k_norm, cos, sin, seq):
    t = x2d.shape[0]
    n_seq_tiles = seq // PROJ_TM
    const = lambda i, j: (0, 0)
    return pl.pallas_call(
        _proj_body,
        grid=(t // PROJ_TM, D_MAIN // PROJ_TN),
        in_specs=[
            pl.BlockSpec((PROJ_TM, D_MODEL), lambda i, j: (i, 0)),
            pl.BlockSpec((1, D_MODEL), const),
            pl.BlockSpec((D_MODEL, PROJ_TN), lambda i, j: (0, j)),
            pl.BlockSpec((D_MODEL, LANES), const),
            pl.BlockSpec((LANES, GLA_QK), const),
            pl.BlockSpec((1, GLA_QK), const),
            pl.BlockSpec((1, ATT_HD), const),
            pl.BlockSpec((1, ATT_HD), const),
            pl.BlockSpec((PROJ_TM, ATT_HD), lambda i, j: (i % n_seq_tiles, 0)),
            pl.BlockSpec((PROJ_TM, ATT_HD), lambda i, j: (i % n_seq_tiles, 0)),
        ],
        out_specs=[
            pl.BlockSpec((PROJ_TM, PROJ_TN), lambda i, j: (i, j)),
            pl.BlockSpec((PROJ_TM, GLA_QK), lambda i, j: (i, 0)),
        ],
        out_shape=[
            jax.ShapeDtypeStruct((t, D_MAIN), BF16),
            jax.ShapeDtypeStruct((t, GLA_QK), F32),
        ],
        scratch_shapes=[pltpu.VMEM((PROJ_TM, D_MODEL), BF16)],
        compiler_params=_params(("parallel", "arbitrary")),
        name="in_proj",
    )(x2d, gain, w_main, w_low, gate_up, gate_bias, q_norm, k_norm, cos, sin)


GLA_LS = 512
GLA_GROUP = 256


def _gla_consts():
    i = np.arange(GLA_GROUP)
    same = (i[:, None] // GLA_CHUNK) == (i[None, :] // GLA_CHUNK)
    tri = (same & (i[None, :] <= i[:, None])).astype(np.float32)
    return jnp.asarray(tri, BF16), jnp.asarray(tri)


def _gla_body(q_ref, k_ref, v_ref, r_ref, la_ref, gain_ref, csum_ref, tri_ref, o_ref, st_ref):
    @pl.when(pl.program_id(1) == 0)
    def _():
        st_ref[...] = jnp.zeros_like(st_ref)

    heads = range(GLA_HEADS)
    chunks = [slice(c * GLA_CHUNK, (c + 1) * GLA_CHUNK) for c in range(GLA_GROUP // GLA_CHUNK)]

    def group(g, carry):
        rows = pl.ds(pl.multiple_of(g * GLA_GROUP, GLA_GROUP), GLA_GROUP)
        ck = [slice(h * GLA_DK, (h + 1) * GLA_DK) for h in heads]
        cv = [slice(h * GLA_DV, (h + 1) * GLA_DV) for h in heads]
        csum = csum_ref[...]
        tri = tri_ref[...] > 0.5

        b = []
        for h in heads:
            la = la_ref[0, rows, ck[h]]
            la_hi = la.astype(BF16)
            la_lo = (la - la_hi.astype(F32)).astype(BF16)
            b.append(_dot(csum, la_hi) + _dot(csum, la_lo))

        def chunk_row(x, row):
            return jnp.concatenate(
                [jnp.broadcast_to(x[cr.start + row:cr.start + row + 1, :], (GLA_CHUNK, x.shape[1]))
                 for cr in chunks], axis=0)

        b_ref = [chunk_row(b[h], GLA_CHUNK // 2 - 1) for h in heads]
        b_last = [chunk_row(b[h], GLA_CHUNK - 1) for h in heads]

        q = [q_ref[0, rows, ck[h]].astype(F32) * (GLA_DK ** -0.5) for h in heads]
        k = [k_ref[0, rows, ck[h]].astype(F32) for h in heads]
        v = [v_ref[0, rows, cv[h]].astype(BF16) for h in heads]
        q_intra = [(q[h] * jnp.exp(b[h] - b_ref[h])).astype(BF16) for h in heads]
        k_intra = [(k[h] * jnp.exp(b_ref[h] - b[h])).astype(BF16) for h in heads]
        q_inter = [(q[h] * jnp.exp(b[h])).astype(BF16) for h in heads]
        k_state = [(k[h] * jnp.exp(b_last[h] - b[h])).astype(BF16) for h in heads]

        scores = [jnp.where(tri, _dot_nt(q_intra[h], k_intra[h]), 0.0).astype(BF16) for h in heads]
        kv = [[_dot_tn(v[h][cr], k_state[h][cr]) for cr in chunks] for h in heads]
        o = [_dot(scores[h], v[h]) for h in heads]

        st = [st_ref[h] for h in heads]
        o_inter = [[] for _ in heads]
        for c, cr in enumerate(chunks):
            for h in heads:
                o_inter[h].append(_dot_nt(q_inter[h][cr], st[h].astype(BF16)))
                decay = jnp.exp(b_last[h][c * GLA_CHUNK:c * GLA_CHUNK + 1, :])
                st[h] = st[h] * decay + kv[h][c]

        gain = gain_ref[...]
        for h in heads:
            st_ref[h] = st[h]
            oh = o[h] + jnp.concatenate(o_inter[h], axis=0)
            oh = oh * _rms_scale(oh) * gain
            r = r_ref[0, rows, cv[h]].astype(F32)
            o_ref[0, rows, cv[h]] = (oh * (r * jax.nn.sigmoid(r))).astype(o_ref.dtype)
        return carry

    lax.fori_loop(0, GLA_LS // GLA_GROUP, group, 0)


def _gla(z3, la3, gain):
    bsz, seq, _ = z3.shape
    csum, tri = _gla_consts()
    const = lambda b, s: (0, 0)
    return pl.pallas_call(
        _gla_body,
        grid=(bsz, seq // GLA_LS),
        in_specs=[
            pl.BlockSpec((1, GLA_LS, GLA_QK), lambda b, s: (b, s, COL_QG // GLA_QK)),
            pl.BlockSpec((1, GLA_LS, GLA_QK), lambda b, s: (b, s, COL_KG // GLA_QK)),
            pl.BlockSpec((1, GLA_LS, GLA_WIDTH), lambda b, s: (b, s, COL_VG // GLA_WIDTH)),
            pl.BlockSpec((1, GLA_LS, GLA_WIDTH), lambda b, s: (b, s, COL_RG // GLA_WIDTH)),
            pl.BlockSpec((1, GLA_LS, GLA_QK), lambda b, s: (b, s, 0)),
            pl.BlockSpec((1, GLA_DV), const),
            pl.BlockSpec((GLA_GROUP, GLA_GROUP), const),
            pl.BlockSpec((GLA_GROUP, GLA_GROUP), const),
        ],
        out_specs=pl.BlockSpec((1, GLA_LS, GLA_WIDTH), lambda b, s: (b, s, 0)),
        out_shape=jax.ShapeDtypeStruct((bsz, seq, GLA_WIDTH), BF16),
        scratch_shapes=[pltpu.VMEM((GLA_HEADS, GLA_DV, GLA_DK), F32)],
        compiler_params=_params(("parallel", "arbitrary")),
        name="gla",
    )(z3, z3, z3, z3, la3, gain, csum, tri)


NEG = -1e30
ATT_MERGE_ROWS = 512
ATT_UNROLL = 8


def _att_consts():
    qi = np.arange(ATT_BLOCK)[:, None]
    kj = np.arange(2 * ATT_BLOCK)[None, :]
    first = kj <= qi
    dist = qi + ATT_BLOCK - kj
    band = (dist >= 0) & (dist <= ATT_BLOCK)
    bias = np.where(np.stack([first, band]), 0.0, NEG).astype(np.float32)
    return jnp.asarray(bias), jnp.ones((2 * ATT_BLOCK, LANES), BF16)


def _att_body(qb_ref, kb_ref, vb_ref, bias_ref, ones_ref, o_ref, qkv_ref, ob_ref, lse_ref, *, seq):
    def widen(i, carry):
        rs = pl.ds(pl.multiple_of(i * ATT_MERGE_ROWS, ATT_MERGE_ROWS), ATT_MERGE_ROWS)
        for a, src in enumerate((qb_ref, kb_ref, vb_ref)):
            qkv_ref[a, rs, :] = src[0, rs, :].astype(F32)
        return carry

    lax.fori_loop(0, seq // ATT_MERGE_ROWS, widen, 0)

    def blocks(br, r, specs):
        def rows(start, size):
            return pl.ds(start, size) if r == 1 else pl.ds(start, size, stride=r)

        ss, vs = [], []
        for q_start, k_start, n_keys, bias in specs:
            q = qkv_ref[0, rows(q_start, ATT_BLOCK), :].astype(BF16)
            kk = qkv_ref[1, rows(k_start, n_keys), :].astype(BF16)
            ss.append(_dot_nt(q, kk) + bias)
            vv = qkv_ref[2, rows(k_start, n_keys), :].astype(BF16)
            vs.append(jnp.concatenate([vv, ones_ref[:n_keys, :]], axis=1))
        ms = [jnp.max(s, axis=-1, keepdims=True) for s in ss]
        ps = [jnp.exp2(s - m).astype(BF16) for s, m in zip(ss, ms)]
        ols = [_dot(p, v) for p, v in zip(ps, vs)]
        for (q_start, _, _, _), m, ol in zip(specs, ms, ols):
            o, l = ol[:, :ATT_HD], ol[:, ATT_HD:]
            ob_ref[br, rows(q_start, ATT_BLOCK), :] = o / l
            lse_ref[br, rows(q_start, ATT_BLOCK), :] = m + jnp.log2(l)

    for br, (window, r) in enumerate(DIL_CONFIGS):
        assert window // r == ATT_BLOCK
        n_blocks = seq // (ATT_BLOCK * r)
        span = ATT_BLOCK * r
        u_c = min(r, ATT_UNROLL)
        u_n = ATT_UNROLL // u_c
        assert n_blocks >= 2 and r % u_c == 0 and n_blocks % u_n == 0
        c_groups = r // u_c

        if u_n == 1:
            def first_row(i, carry, br=br, r=r, u_c=u_c):
                cs = [i * u_c + u for u in range(u_c)]
                blocks(br, r, [(c, c, ATT_BLOCK, bias_ref[0, :, :ATT_BLOCK]) for c in cs])
                return carry

            def later_rows(i, carry, br=br, r=r, span=span, u_c=u_c, c_groups=c_groups):
                n = 1 + i // c_groups
                cs = [(i % c_groups) * u_c + u for u in range(u_c)]
                blocks(br, r, [(c + n * span, c + (n - 1) * span, 2 * ATT_BLOCK, bias_ref[1])
                               for c in cs])
                return carry

            lax.fori_loop(0, c_groups, first_row, 0)
            lax.fori_loop(0, (n_blocks - 1) * c_groups, later_rows, 0)
        else:
            assert c_groups == 1
            def rows_iter(i, carry, br=br, r=r, span=span, u_n=u_n):
                specs = []
                for un in range(u_n):
                    n = i * u_n + un
                    if un == 0:
                        bias = bias_ref[jnp.minimum(n, 1)]
                        k_row = jnp.maximum(n - 1, 0) * span
                    else:
                        bias = bias_ref[1]
                        k_row = (n - 1) * span
                    specs += [(c + n * span, c + k_row, 2 * ATT_BLOCK, bias) for c in range(r)]
                blocks(br, r, specs)
                return carry

            lax.fori_loop(0, n_blocks // u_n, rows_iter, 0)

    def merge(i, carry):
        rs = pl.ds(pl.multiple_of(i * ATT_MERGE_ROWS, ATT_MERGE_ROWS), ATT_MERGE_ROWS)
        lses = [lse_ref[b, rs, :] for b in range(len(DIL_CONFIGS))]
        m = functools.reduce(jnp.maximum, lses)
        ws = [jnp.exp2(x - m) for x in lses]
        den = functools.reduce(lambda a, b: a + b, ws)
        num = functools.reduce(lambda a, b: a + b, [w * ob_ref[b, rs, :] for b, w in enumerate(ws)])
        o_ref[0, rs, :] = (num / den).astype(o_ref.dtype)
        return carry

    lax.fori_loop(0, seq // ATT_MERGE_ROWS, merge, 0)


def _dil_attn(z3):
    bsz, seq, _ = z3.shape
    bias, ones = _att_consts()
    nb = len(DIL_CONFIGS)
    return pl.pallas_call(
        functools.partial(_att_body, seq=seq),
        grid=(bsz, ATT_HEADS),
        in_specs=[
            pl.BlockSpec((1, seq, ATT_HD), lambda b, h: (b, 0, COL_QA // ATT_HD + h)),
            pl.BlockSpec((1, seq, ATT_HD), lambda b, h: (b, 0, COL_KA // ATT_HD + h)),
            pl.BlockSpec((1, seq, ATT_HD), lambda b, h: (b, 0, COL_VA // ATT_HD + h)),
            pl.BlockSpec((2, ATT_BLOCK, 2 * ATT_BLOCK), lambda b, h: (0, 0, 0)),
            pl.BlockSpec((2 * ATT_BLOCK, LANES), lambda b, h: (0, 0)),
        ],
        out_specs=pl.BlockSpec((1, seq, ATT_HD), lambda b, h: (b, 0, h)),
        out_shape=jax.ShapeDtypeStruct((bsz, seq, ATT_WIDTH), BF16),
        scratch_shapes=[pltpu.VMEM((3, seq, ATT_HD), F32),
                        pltpu.VMEM((nb, seq, ATT_HD), F32),
                        pltpu.VMEM((nb, seq, ATT_HD), F32)],
        compiler_params=_params(("parallel", "parallel")),
        name="dil_attn",
    )(z3, z3, z3, bias, ones)


OUT_TM = 512
OUT_TN = D_MODEL


def _out_body(x_ref, og_ref, oa_ref, wg_ref, wa_ref, o_ref):
    o_ref[...] = x_ref[...] + _dot(og_ref[...], wg_ref[...]) + _dot(oa_ref[...], wa_ref[...])


def _out_proj(x2d, og, oa, w_top, w_bot):
    t = x2d.shape[0]
    return pl.pallas_call(
        _out_body,
        grid=(t // OUT_TM, D_MODEL // OUT_TN),
        in_specs=[
            pl.BlockSpec((OUT_TM, OUT_TN), lambda i, j: (i, j)),
            pl.BlockSpec((OUT_TM, GLA_WIDTH), lambda i, j: (i, 0)),
            pl.BlockSpec((OUT_TM, ATT_WIDTH), lambda i, j: (i, 0)),
            pl.BlockSpec((GLA_WIDTH, OUT_TN), lambda i, j: (0, j)),
            pl.BlockSpec((ATT_WIDTH, OUT_TN), lambda i, j: (0, j)),
        ],
        out_specs=pl.BlockSpec((OUT_TM, OUT_TN), lambda i, j: (i, j)),
        out_shape=jax.ShapeDtypeStruct((t, D_MODEL), F32),
        compiler_params=_params(("parallel", "arbitrary")),
        name="out_proj",
    )(x2d, og, oa, w_top, w_bot)


def _rope_tables(seq):
    half = ATT_HD // 2
    inv_freq = 1.0 / (ROPE_THETA ** (jnp.arange(half, dtype=F32) / half))
    ang = jnp.arange(seq, dtype=F32)[:, None] * inv_freq[None, :]
    cos, sin = jnp.cos(ang), jnp.sin(ang)
    return jnp.concatenate([cos, cos], axis=1), jnp.concatenate([-sin, sin], axis=1)


def _split_w_in(w_in):
    sizes = (GLA_QK, GLA_QK, GLA_WIDTH, GLA_GATE_RANK, GLA_WIDTH, ATT_WIDTH, ATT_WIDTH, ATT_WIDTH)
    offs = np.concatenate([[0], np.cumsum(sizes)])
    w_in = w_in.astype(BF16)
    seg = [w_in[:, offs[i]:offs[i + 1]] for i in range(len(sizes))]
    q_g, k_g, v_g, g_low, r_g, q_a, k_a, v_a = seg
    w_main = jnp.concatenate([q_g, k_g, v_g, r_g, q_a, k_a, v_a], axis=1)
    w_low = jnp.pad(g_low, ((0, 0), (0, LANES - GLA_GATE_RANK)))
    return w_main, w_low


def kernel(x, ffn1_norm, ffn1_w_gate, ffn1_w_up, ffn1_w_down, mix_norm, w_in, gla_gate_up, gla_gate_bias, gla_out_norm, att_q_norm, att_k_norm, w_out, ffn2_norm, ffn2_w_gate, ffn2_w_up, ffn2_w_down):
    bsz, seq, d = x.shape
    depth = ffn1_norm.shape[0]
    t = bsz * seq
    cos, sin = _rope_tables(seq)
    x2d = x.reshape(t, d)
    for l in range(depth):
        x2d = _ffn(x2d, ffn1_norm[l][None], ffn1_w_gate[l].astype(BF16), ffn1_w_up[l].astype(BF16),
                   ffn1_w_down[l].astype(BF16))

        w_main, w_low = _split_w_in(w_in[l])
        gate_up = jnp.pad(gla_gate_up[l], ((0, LANES - GLA_GATE_RANK), (0, 0))).astype(BF16)
        z, la = _in_proj(x2d, mix_norm[l][None], w_main, w_low, gate_up, gla_gate_bias[l][None],
                         att_q_norm[l][None], att_k_norm[l][None], cos, sin, seq)
        z3 = z.reshape(bsz, seq, D_MAIN)
        o_g = _gla(z3, la.reshape(bsz, seq, GLA_QK), gla_out_norm[l][None])
        o_a = _dil_attn(z3)

        w_o = w_out[l].astype(BF16)
        x2d = _out_proj(x2d, o_g.reshape(t, GLA_WIDTH), o_a.reshape(t, ATT_WIDTH),
                        w_o[:GLA_WIDTH], w_o[GLA_WIDTH:])

        x2d = _ffn(x2d, ffn2_norm[l][None], ffn2_w_gate[l].astype(BF16), ffn2_w_up[l].astype(BF16),
                   ffn2_w_down[l].astype(BF16))
    return x2d.reshape(bsz, seq, d)
```

```python
import functools

import numpy as np
import jax
import jax.numpy as jnp
from jax import lax
from jax.experimental import pallas as pl
from jax.experimental.pallas import tpu as pltpu

F32 = jnp.float32
BF16 = jnp.bfloat16

D_MODEL = 2048
D_FF = 5632
EPS = 1e-6
FFN_RESIDUAL_WEIGHT = 0.5

GLA_HEADS = 4
GLA_DK = 128
GLA_DV = 256
GLA_QK = GLA_HEADS * GLA_DK
GLA_WIDTH = GLA_HEADS * GLA_DV
GLA_GATE_RANK = 16
GLA_GATE_TAU = 16.0
GLA_CHUNK = 64

ATT_HD = 128
ATT_HEADS = 8
ATT_WIDTH = ATT_HEADS * ATT_HD
DIL_CONFIGS = ((128, 1), (512, 4), (2048, 16))
ATT_BLOCK = 128
ROPE_THETA = 10000.0

VMEM_LIMIT_BYTES = 56 * 1024 * 1024
LANES = 128

COL_QG, COL_KG, COL_VG, COL_RG = 0, 512, 1024, 2048
COL_QA, COL_KA, COL_VA = 3072, 4096, 5120
D_MAIN = 6144
PROJ_TN = 2048


def _params(sem):
    return pltpu.CompilerParams(dimension_semantics=sem, vmem_limit_bytes=VMEM_LIMIT_BYTES)


def _rms_scale(x):
    return lax.rsqrt(jnp.mean(x * x, axis=-1, keepdims=True) + EPS)


def _dot(a, b):
    return jnp.dot(a, b, preferred_element_type=F32)


def _dot_nt(a, b):
    return lax.dot_general(a, b, (((1,), (1,)), ((), ())), preferred_element_type=F32)


def _dot_tn(a, b):
    return lax.dot_general(a, b, (((0,), (0,)), ((), ())), preferred_element_type=F32)


FFN_TM = 1024
FFN_TF = 512


def _ffn_body(x_ref, gain_ref, wg_ref, wu_ref, wd_ref, o_ref, h_ref, rs_ref):
    j = pl.program_id(1)

    def swiglu_down():
        h = h_ref[...]
        rs = jnp.tile(rs_ref[...], (1, FFN_TF // LANES))
        g = _dot(h, wg_ref[...]) * rs
        u = _dot(h, wu_ref[...]) * rs
        a = (g * jax.nn.sigmoid(g) * u * FFN_RESIDUAL_WEIGHT).astype(BF16)
        return _dot(a, wd_ref[...])

    @pl.when(j == 0)
    def _():
        x = x_ref[...]
        h_ref[...] = (x * gain_ref[...]).astype(BF16)
        rs_ref[...] = jnp.broadcast_to(_rms_scale(x), rs_ref.shape)
        o_ref[...] = x + swiglu_down()

    @pl.when(j > 0)
    def _():
        o_ref[...] += swiglu_down()


def _ffn(x2d, gain, wg, wu, wd):
    t = x2d.shape[0]
    return pl.pallas_call(
        _ffn_body,
        grid=(t // FFN_TM, D_FF // FFN_TF),
        in_specs=[
            pl.BlockSpec((FFN_TM, D_MODEL), lambda i, j: (i, 0)),
            pl.BlockSpec((1, D_MODEL), lambda i, j: (0, 0)),
            pl.BlockSpec((D_MODEL, FFN_TF), lambda i, j: (0, j)),
            pl.BlockSpec((D_MODEL, FFN_TF), lambda i, j: (0, j)),
            pl.BlockSpec((FFN_TF, D_MODEL), lambda i, j: (j, 0)),
        ],
        out_specs=pl.BlockSpec((FFN_TM, D_MODEL), lambda i, j: (i, 0)),
        out_shape=jax.ShapeDtypeStruct((t, D_MODEL), F32),
        scratch_shapes=[pltpu.VMEM((FFN_TM, D_MODEL), BF16),
                        pltpu.VMEM((FFN_TM, LANES), F32)],
        compiler_params=_params(("parallel", "arbitrary")),
        name="ffn",
    )(x2d, gain, wg, wu, wd)


PROJ_TM = 512
PROJ_PIECE = 256
LOG2E = 1.4426950408889634
ATT_Q_SCALE = ATT_HD ** -0.5 * LOG2E


def _log_sigmoid(x):
    return jnp.minimum(x, 0.0) - jnp.log(1.0 + jnp.exp(-jnp.abs(x)))


def _rope_heads(z, gain, cos, sin):
    outs = []
    for hd in range(z.shape[1] // ATT_HD):
        zh = z[:, hd * ATT_HD:(hd + 1) * ATT_HD]
        zh = zh * _rms_scale(zh) * gain
        outs.append(zh * cos + pltpu.roll(zh, ATT_HD // 2, 1) * sin)
    return jnp.concatenate(outs, axis=1)


def _proj_body(x_ref, gain_ref, w_ref, wlow_ref, gup_ref, gbias_ref, qn_ref, kn_ref,
               cos_ref, sin_ref, *rest):
    n_cast = (len(rest) - 4) // 2
    cast_in, (z_ref, la_ref), cast_out = rest[:n_cast], rest[n_cast:n_cast + 2], rest[n_cast + 2:-2]
    h_ref, rs_ref = rest[-2:]
    j = pl.program_id(1)

    def prologue():
        x = x_ref[...]
        h_ref[...] = (x * gain_ref[...]).astype(BF16)
        rs = jnp.broadcast_to(_rms_scale(x), rs_ref.shape)
        rs_ref[...] = rs
        g_low = _dot(h_ref[...], wlow_ref[...])
        logit = (_dot(g_low.astype(BF16), gup_ref[...]) * jnp.tile(rs, (1, GLA_QK // LANES))
                 + gbias_ref[...])
        la_ref[...] = _log_sigmoid(logit) * (1.0 / GLA_GATE_TAU)

    def piece(c0, gain):
        cols = slice(c0, c0 + PROJ_PIECE)
        z = _dot(h_ref[...], w_ref[:, cols]) * jnp.tile(rs_ref[...], (1, PROJ_PIECE // LANES))
        if gain is not None:
            z = _rope_heads(z, gain, cos_ref[...], sin_ref[...])
        z_ref[:, cols] = z.astype(z_ref.dtype)

    def has_epilogue(col):
        return COL_QA <= col < COL_VA

    def piece_gain(col):
        if not has_epilogue(col):
            return None
        return qn_ref[...] * ATT_Q_SCALE if col < COL_KA else kn_ref[...]

    for step in range(D_MAIN // PROJ_TN):
        @pl.when(j == step)
        def _(step=step):
            if step == 0:
                prologue()
            starts = list(range(0, PROJ_TN, PROJ_PIECE))
            starts.sort(key=lambda c0: not has_epilogue(step * PROJ_TN + c0))
            for c0 in starts:
                piece(c0, piece_gain(step * PROJ_TN + c0))
            if step == D_MAIN // PROJ_TN - 1:
                for src, dst in zip(cast_in, cast_out):
                    dst[...] = src[...].astype(dst.dtype)


def _in_proj(x2d, gain, w_main, w_low, gate_up, gate_bias, q_norm, k_norm, cos, sin, seq, to_cast):
    t = x2d.shape[0]
    n_seq_tiles = seq // PROJ_TM
    n_tiles = t // PROJ_TM
    const = lambda i, j: (0, 0)
    cast_specs = []
    for w in to_cast:
        assert w.shape[0] % (16 * n_tiles) == 0 and w.shape[1] % LANES == 0
        cast_specs.append(pl.BlockSpec((w.shape[0] // n_tiles, w.shape[1]), lambda i, j: (i, 0)))
    return pl.pallas_call(
        _proj_body,
        grid=(n_tiles, D_MAIN // PROJ_TN),
        in_specs=[
            pl.BlockSpec((PROJ_TM, D_MODEL), lambda i, j: (i, 0)),
            pl.BlockSpec((1, D_MODEL), const),
            pl.BlockSpec((D_MODEL, PROJ_TN), lambda i, j: (0, j)),
            pl.BlockSpec((D_MODEL, LANES), const),
            pl.BlockSpec((LANES, GLA_QK), const),
            pl.BlockSpec((1, GLA_QK), const),
            pl.BlockSpec((1, ATT_HD), const),
            pl.BlockSpec((1, ATT_HD), const),
            pl.BlockSpec((PROJ_TM, ATT_HD), lambda i, j: (i % n_seq_tiles, 0)),
            pl.BlockSpec((PROJ_TM, ATT_HD), lambda i, j: (i % n_seq_tiles, 0)),
        ] + cast_specs,
        out_specs=[
            pl.BlockSpec((PROJ_TM, PROJ_TN), lambda i, j: (i, j)),
            pl.BlockSpec((PROJ_TM, GLA_QK), lambda i, j: (i, 0)),
        ] + cast_specs,
        out_shape=[
            jax.ShapeDtypeStruct((t, D_MAIN), BF16),
            jax.ShapeDtypeStruct((t, GLA_QK), F32),
        ] + [jax.ShapeDtypeStruct(w.shape, BF16) for w in to_cast],
        scratch_shapes=[pltpu.VMEM((PROJ_TM, D_MODEL), BF16),
                        pltpu.VMEM((PROJ_TM, LANES), F32)],
        compiler_params=_params(("parallel", "arbitrary")),
        name="in_proj",
    )(x2d, gain, w_main, w_low, gate_up, gate_bias, q_norm, k_norm, cos, sin, *to_cast)


GLA_LS = 512
GLA_GROUP = 256


def _gla_consts():
    i = np.arange(GLA_GROUP)
    same = (i[:, None] // GLA_CHUNK) == (i[None, :] // GLA_CHUNK)
    tri = (same & (i[None, :] <= i[:, None])).astype(np.float32)
    return jnp.asarray(tri, BF16), jnp.asarray(tri)


def _gla_body(q_ref, k_ref, v_ref, r_ref, la_ref, gain_ref, csum_ref, tri_ref, o_ref, st_ref):
    @pl.when(pl.program_id(1) == 0)
    def _():
        st_ref[...] = jnp.zeros_like(st_ref)

    heads = range(GLA_HEADS)
    chunks = [slice(c * GLA_CHUNK, (c + 1) * GLA_CHUNK) for c in range(GLA_GROUP // GLA_CHUNK)]

    def group(g, carry):
        rows = pl.ds(pl.multiple_of(g * GLA_GROUP, GLA_GROUP), GLA_GROUP)
        ck = [slice(h * GLA_DK, (h + 1) * GLA_DK) for h in heads]
        cv = [slice(h * GLA_DV, (h + 1) * GLA_DV) for h in heads]
        csum = csum_ref[...]
        tri = tri_ref[...] > 0.5

        b = []
        for h in heads:
            la = la_ref[0, rows, ck[h]]
            la_hi = la.astype(BF16)
            la_lo = (la - la_hi.astype(F32)).astype(BF16)
            b.append(_dot(csum, la_hi) + _dot(csum, la_lo))

        def chunk_row(x, row):
            return jnp.concatenate(
                [jnp.broadcast_to(x[cr.start + row:cr.start + row + 1, :], (GLA_CHUNK, x.shape[1]))
                 for cr in chunks], axis=0)

        b_ref = [chunk_row(b[h], GLA_CHUNK // 2 - 1) for h in heads]
        b_last = [chunk_row(b[h], GLA_CHUNK - 1) for h in heads]

        q = [q_ref[0, rows, ck[h]].astype(F32) * (GLA_DK ** -0.5) for h in heads]
        k = [k_ref[0, rows, ck[h]].astype(F32) for h in heads]
        v = [v_ref[0, rows, cv[h]].astype(BF16) for h in heads]
        q_intra = [(q[h] * jnp.exp(b[h] - b_ref[h])).astype(BF16) for h in heads]
        k_intra = [(k[h] * jnp.exp(b_ref[h] - b[h])).astype(BF16) for h in heads]
        q_inter = [(q[h] * jnp.exp(b[h])).astype(BF16) for h in heads]
        k_state = [(k[h] * jnp.exp(b_last[h] - b[h])).astype(BF16) for h in heads]

        scores = [jnp.where(tri, _dot_nt(q_intra[h], k_intra[h]), 0.0).astype(BF16) for h in heads]
        kv = [[_dot_tn(v[h][cr], k_state[h][cr]) for cr in chunks] for h in heads]
        o = [_dot(scores[h], v[h]) for h in heads]

        st = [st_ref[h] for h in heads]
        o_inter = [[] for _ in heads]
        for c, cr in enumerate(chunks):
            for h in heads:
                o_inter[h].append(_dot_nt(q_inter[h][cr], st[h].astype(BF16)))
                decay = jnp.exp(b_last[h][c * GLA_CHUNK:c * GLA_CHUNK + 1, :])
                st[h] = st[h] * decay + kv[h][c]

        gain = gain_ref[...]
        for h in heads:
            st_ref[h] = st[h]
            oh = o[h] + jnp.concatenate(o_inter[h], axis=0)
            oh = oh * _rms_scale(oh) * gain
            r = r_ref[0, rows, cv[h]].astype(F32)
            o_ref[0, rows, cv[h]] = (oh * (r * jax.nn.sigmoid(r))).astype(o_ref.dtype)
        return carry

    lax.fori_loop(0, GLA_LS // GLA_GROUP, group, 0)


def _gla(z3, la3, gain):
    bsz, seq, _ = z3.shape
    csum, tri = _gla_consts()
    const = lambda b, s: (0, 0)
    return pl.pallas_call(
        _gla_body,
        grid=(bsz, seq // GLA_LS),
        in_specs=[
            pl.BlockSpec((1, GLA_LS, GLA_QK), lambda b, s: (b, s, COL_QG // GLA_QK)),
            pl.BlockSpec((1, GLA_LS, GLA_QK), lambda b, s: (b, s, COL_KG // GLA_QK)),
            pl.BlockSpec((1, GLA_LS, GLA_WIDTH), lambda b, s: (b, s, COL_VG // GLA_WIDTH)),
            pl.BlockSpec((1, GLA_LS, GLA_WIDTH), lambda b, s: (b, s, COL_RG // GLA_WIDTH)),
            pl.BlockSpec((1, GLA_LS, GLA_QK), lambda b, s: (b, s, 0)),
            pl.BlockSpec((1, GLA_DV), const),
            pl.BlockSpec((GLA_GROUP, GLA_GROUP), const),
            pl.BlockSpec((GLA_GROUP, GLA_GROUP), const),
        ],
        out_specs=pl.BlockSpec((1, GLA_LS, GLA_WIDTH), lambda b, s: (b, s, 0)),
        out_shape=jax.ShapeDtypeStruct((bsz, seq, GLA_WIDTH), BF16),
        scratch_shapes=[pltpu.VMEM((GLA_HEADS, GLA_DV, GLA_DK), F32)],
        compiler_params=_params(("parallel", "arbitrary")),
        name="gla",
    )(z3, z3, z3, z3, la3, gain, csum, tri)


NEG = -1e30
ATT_MERGE_ROWS = 512
ATT_UNROLL = 8


def _att_consts():
    qi = np.arange(ATT_BLOCK)[:, None]
    kj = np.arange(2 * ATT_BLOCK)[None, :]
    first = kj <= qi
    dist = qi + ATT_BLOCK - kj
    band = (dist >= 0) & (dist <= ATT_BLOCK)
    bias = np.where(np.stack([first, band]), 0.0, NEG).astype(np.float32)
    return jnp.asarray(bias), jnp.ones((2 * ATT_BLOCK, LANES), BF16)


def _att_body(qb_ref, kb_ref, vb_ref, bias_ref, ones_ref, o_ref, qkv_ref, ob_ref, lse_ref, *, seq):
    def widen(i, carry):
        rs = pl.ds(pl.multiple_of(i * ATT_MERGE_ROWS, ATT_MERGE_ROWS), ATT_MERGE_ROWS)
        for a, src in enumerate((qb_ref, kb_ref, vb_ref)):
            qkv_ref[a, rs, :] = src[0, rs, :].astype(F32)
        return carry

    lax.fori_loop(0, seq // ATT_MERGE_ROWS, widen, 0)

    def blocks(br, r, specs):
        def rows(start, size):
            return pl.ds(start, size) if r == 1 else pl.ds(start, size, stride=r)

        ss, vs = [], []
        for q_start, k_start, n_keys, bias in specs:
            q = qkv_ref[0, rows(q_start, ATT_BLOCK), :].astype(BF16)
            kk = qkv_ref[1, rows(k_start, n_keys), :].astype(BF16)
            ss.append(_dot_nt(q, kk) + bias)
            vv = qkv_ref[2, rows(k_start, n_keys), :].astype(BF16)
            vs.append(jnp.concatenate([vv, ones_ref[:n_keys, :]], axis=1))
        ms = [jnp.max(s, axis=-1, keepdims=True) for s in ss]
        ps = [jnp.exp2(s - m).astype(BF16) for s, m in zip(ss, ms)]
        ols = [_dot(p, v) for p, v in zip(ps, vs)]
        for (q_start, _, _, _), m, ol in zip(specs, ms, ols):
            o, l = ol[:, :ATT_HD], ol[:, ATT_HD:]
            ob_ref[br, rows(q_start, ATT_BLOCK), :] = o / l
            lse_ref[br, rows(q_start, ATT_BLOCK), :] = m + jnp.log2(l)

    for br, (window, r) in enumerate(DIL_CONFIGS):
        assert window // r == ATT_BLOCK
        n_blocks = seq // (ATT_BLOCK * r)
        span = ATT_BLOCK * r
        u_c = min(r, ATT_UNROLL)
        u_n = ATT_UNROLL // u_c
        assert n_blocks >= 2 and r % u_c == 0 and n_blocks % u_n == 0
        c_groups = r // u_c

        if u_n == 1:
            def first_row(i, carry, br=br, r=r, u_c=u_c):
                cs = [i * u_c + u for u in range(u_c)]
                blocks(br, r, [(c, c, ATT_BLOCK, bias_ref[0, :, :ATT_BLOCK]) for c in cs])
                return carry

            def later_rows(i, carry, br=br, r=r, span=span, u_c=u_c, c_groups=c_groups):
                n = 1 + i // c_groups
                cs = [(i % c_groups) * u_c + u for u in range(u_c)]
                blocks(br, r, [(c + n * span, c + (n - 1) * span, 2 * ATT_BLOCK, bias_ref[1])
                               for c in cs])
                return carry

            lax.fori_loop(0, c_groups, first_row, 0)
            lax.fori_loop(0, (n_blocks - 1) * c_groups, later_rows, 0)
        else:
            assert c_groups == 1
            def rows_iter(i, carry, br=br, r=r, span=span, u_n=u_n):
                specs = []
                for un in range(u_n):
                    n = i * u_n + un
                    if un == 0:
                        bias = bias_ref[jnp.minimum(n, 1)]
                        k_row = jnp.maximum(n - 1, 0) * span
                    else:
                        bias = bias_ref[1]
                        k_row = (n - 1) * span
                    specs += [(c + n * span, c + k_row, 2 * ATT_BLOCK, bias) for c in range(r)]
                blocks(br, r, specs)
                return carry

            lax.fori_loop(0, n_blocks // u_n, rows_iter, 0)

    def merge(i, carry):
        rs = pl.ds(pl.multiple_of(i * ATT_MERGE_ROWS, ATT_MERGE_ROWS), ATT_MERGE_ROWS)
        lses = [lse_ref[b, rs, :] for b in range(len(DIL_CONFIGS))]
        m = functools.reduce(jnp.maximum, lses)
        ws = [jnp.exp2(x - m) for x in lses]
        den = functools.reduce(lambda a, b: a + b, ws)
        num = functools.reduce(lambda a, b: a + b, [w * ob_ref[b, rs, :] for b, w in enumerate(ws)])
        o_ref[0, rs, :] = (num / den).astype(o_ref.dtype)
        return carry

    lax.fori_loop(0, seq // ATT_MERGE_ROWS, merge, 0)


def _dil_attn(z3):
    bsz, seq, _ = z3.shape
    bias, ones = _att_consts()
    nb = len(DIL_CONFIGS)
    return pl.pallas_call(
        functools.partial(_att_body, seq=seq),
        grid=(bsz, ATT_HEADS),
        in_specs=[
            pl.BlockSpec((1, seq, ATT_HD), lambda b, h: (b, 0, COL_QA // ATT_HD + h)),
            pl.BlockSpec((1, seq, ATT_HD), lambda b, h: (b, 0, COL_KA // ATT_HD + h)),
            pl.BlockSpec((1, seq, ATT_HD), lambda b, h: (b, 0, COL_VA // ATT_HD + h)),
            pl.BlockSpec((2, ATT_BLOCK, 2 * ATT_BLOCK), lambda b, h: (0, 0, 0)),
            pl.BlockSpec((2 * ATT_BLOCK, LANES), lambda b, h: (0, 0)),
        ],
        out_specs=pl.BlockSpec((1, seq, ATT_HD), lambda b, h: (b, 0, h)),
        out_shape=jax.ShapeDtypeStruct((bsz, seq, ATT_WIDTH), BF16),
        scratch_shapes=[pltpu.VMEM((3, seq, ATT_HD), F32),
                        pltpu.VMEM((nb, seq, ATT_HD), F32),
                        pltpu.VMEM((nb, seq, ATT_HD), F32)],
        compiler_params=_params(("parallel", "parallel")),
        name="dil_attn",
    )(z3, z3, z3, bias, ones)


OUT_TM = 512
OUT_TN = D_MODEL


def _out_body(x_ref, og_ref, oa_ref, wg_ref, wa_ref, o_ref):
    o_ref[...] = x_ref[...] + _dot(og_ref[...], wg_ref[...]) + _dot(oa_ref[...], wa_ref[...])


def _out_proj(x2d, og, oa, w_top, w_bot):
    t = x2d.shape[0]
    return pl.pallas_call(
        _out_body,
        grid=(t // OUT_TM, D_MODEL // OUT_TN),
        in_specs=[
            pl.BlockSpec((OUT_TM, OUT_TN), lambda i, j: (i, j)),
            pl.BlockSpec((OUT_TM, GLA_WIDTH), lambda i, j: (i, 0)),
            pl.BlockSpec((OUT_TM, ATT_WIDTH), lambda i, j: (i, 0)),
            pl.BlockSpec((GLA_WIDTH, OUT_TN), lambda i, j: (0, j)),
            pl.BlockSpec((ATT_WIDTH, OUT_TN), lambda i, j: (0, j)),
        ],
        out_specs=pl.BlockSpec((OUT_TM, OUT_TN), lambda i, j: (i, j)),
        out_shape=jax.ShapeDtypeStruct((t, D_MODEL), F32),
        compiler_params=_params(("parallel", "arbitrary")),
        name="out_proj",
    )(x2d, og, oa, w_top, w_bot)


def _rope_tables(seq):
    half = ATT_HD // 2
    inv_freq = 1.0 / (ROPE_THETA ** (jnp.arange(half, dtype=F32) / half))
    ang = jnp.arange(seq, dtype=F32)[:, None] * inv_freq[None, :]
    cos, sin = jnp.cos(ang), jnp.sin(ang)
    return jnp.concatenate([cos, cos], axis=1), jnp.concatenate([-sin, sin], axis=1)


def _split_w_in(w_in):
    sizes = (GLA_QK, GLA_QK, GLA_WIDTH, GLA_GATE_RANK, GLA_WIDTH, ATT_WIDTH, ATT_WIDTH, ATT_WIDTH)
    offs = np.concatenate([[0], np.cumsum(sizes)])
    w_in = w_in.astype(BF16)
    seg = [w_in[:, offs[i]:offs[i + 1]] for i in range(len(sizes))]
    q_g, k_g, v_g, g_low, r_g, q_a, k_a, v_a = seg
    w_main = jnp.concatenate([q_g, k_g, v_g, r_g, q_a, k_a, v_a], axis=1)
    w_low = jnp.pad(g_low, ((0, 0), (0, LANES - GLA_GATE_RANK)))
    return w_main, w_low


def kernel(x, ffn1_norm, ffn1_w_gate, ffn1_w_up, ffn1_w_down, mix_norm, w_in, gla_gate_up, gla_gate_bias, gla_out_norm, att_q_norm, att_k_norm, w_out, ffn2_norm, ffn2_w_gate, ffn2_w_up, ffn2_w_down):
    bsz, seq, d = x.shape
    depth = ffn1_norm.shape[0]
    t = bsz * seq
    cos, sin = _rope_tables(seq)
    x2d = x.reshape(t, d)
    for l in range(depth):
        x2d = _ffn(x2d, ffn1_norm[l][None], ffn1_w_gate[l].astype(BF16), ffn1_w_up[l].astype(BF16),
                   ffn1_w_down[l].astype(BF16))

        w_main, w_low = _split_w_in(w_in[l])
        gate_up = jnp.pad(gla_gate_up[l], ((0, LANES - GLA_GATE_RANK), (0, 0))).astype(BF16)
        z, la, w2_gate, w2_up, w2_down = _in_proj(
            x2d, mix_norm[l][None], w_main, w_low, gate_up, gla_gate_bias[l][None],
            att_q_norm[l][None], att_k_norm[l][None], cos, sin, seq,
            (ffn2_w_gate[l], ffn2_w_up[l], ffn2_w_down[l]))
        z3 = z.reshape(bsz, seq, D_MAIN)
        o_g = _gla(z3, la.reshape(bsz, seq, GLA_QK), gla_out_norm[l][None])
        o_a = _dil_attn(z3)

        w_o = w_out[l].astype(BF16)
        x2d = _out_proj(x2d, o_g.reshape(t, GLA_WIDTH), o_a.reshape(t, ATT_WIDTH),
                        w_o[:GLA_WIDTH], w_o[GLA_WIDTH:])

        x2d = _ffn(x2d, ffn2_norm[l][None], w2_gate, w2_up, w2_down)
    return x2d.reshape(bsz, seq, d)
```

```python
import functools

import numpy as np
import jax
import jax.numpy as jnp
from jax import lax
from jax.experimental import pallas as pl
from jax.experimental.pallas import tpu as pltpu

F32 = jnp.float32
BF16 = jnp.bfloat16

D_MODEL = 2048
D_FF = 5632
EPS = 1e-6
FFN_RESIDUAL_WEIGHT = 0.5

GLA_HEADS = 4
GLA_DK = 128
GLA_DV = 256
GLA_QK = GLA_HEADS * GLA_DK
GLA_WIDTH = GLA_HEADS * GLA_DV
GLA_GATE_RANK = 16
GLA_GATE_TAU = 16.0
GLA_CHUNK = 64

ATT_HD = 128
ATT_HEADS = 8
ATT_WIDTH = ATT_HEADS * ATT_HD
DIL_CONFIGS = ((128, 1), (512, 4), (2048, 16))
ATT_BLOCK = 128
ROPE_THETA = 10000.0

VMEM_LIMIT_BYTES = 60 * 1024 * 1024
LANES = 128

COL_QG, COL_KG, COL_VG, COL_RG = 0, 512, 1024, 2048
COL_QA, COL_KA, COL_VA = 3072, 4096, 5120
D_MAIN = 6144
PROJ_TN = 2048


def _params(sem):
    return pltpu.CompilerParams(dimension_semantics=sem, vmem_limit_bytes=VMEM_LIMIT_BYTES)


def _rms_scale(x):
    return lax.rsqrt(jnp.mean(x * x, axis=-1, keepdims=True) + EPS)


def _dot(a, b):
    return jnp.dot(a, b, preferred_element_type=F32)


def _dot_nt(a, b):
    return lax.dot_general(a, b, (((1,), (1,)), ((), ())), preferred_element_type=F32)


def _dot_tn(a, b):
    return lax.dot_general(a, b, (((0,), (0,)), ((), ())), preferred_element_type=F32)


FFN_TM = 1024
FFN_TF = 512


W_IN_ROWS = 16


def _ffn_body(*refs, hosted):
    x_ref, gain_ref, wg_ref, wu_ref, wd_ref = refs[:5]
    if hosted:
        pg_ref, pu_ref, pd_ref, pwin_ref, o_ref, og_ref, ou_ref, od_ref, omain_ref, olow_ref = refs[5:15]
    else:
        o_ref = refs[5]
    h_ref, rs_ref = refs[-2:]
    j = pl.program_id(1)

    def prepare_weights():
        for src, dst in ((pg_ref, og_ref), (pu_ref, ou_ref), (pd_ref, od_ref)):
            dst[...] = src[...].astype(dst.dtype)
        w = pwin_ref[...]
        omain_ref[:, :COL_RG] = w[:, :COL_RG].astype(BF16)
        omain_ref[:, COL_RG:] = w[:, COL_RG + GLA_GATE_RANK:].astype(BF16)
        low = w[:, COL_RG:COL_RG + LANES]
        lane = lax.broadcasted_iota(jnp.int32, low.shape, 1)
        olow_ref[...] = jnp.where(lane < GLA_GATE_RANK, low, 0.0).astype(BF16)

    def swiglu_down():
        h = h_ref[...]
        rs = jnp.tile(rs_ref[...], (1, FFN_TF // LANES))
        g = _dot(h, wg_ref[...]) * rs
        u = _dot(h, wu_ref[...]) * rs
        a = (g * jax.nn.sigmoid(g) * u * FFN_RESIDUAL_WEIGHT).astype(BF16)
        if hosted:
            prepare_weights()
        return _dot(a, wd_ref[...])

    @pl.when(j == 0)
    def _():
        x = x_ref[...]
        h_ref[...] = (x * gain_ref[...]).astype(BF16)
        rs_ref[...] = jnp.broadcast_to(_rms_scale(x), rs_ref.shape)
        o_ref[...] = x + swiglu_down()

    @pl.when(j > 0)
    def _():
        o_ref[...] += swiglu_down()


def _ffn(x2d, gain, wg, wu, wd, prepare=None):
    t = x2d.shape[0]
    n_i, n_j = t // FFN_TM, D_FF // FFN_TF
    in_specs = [
        pl.BlockSpec((FFN_TM, D_MODEL), lambda i, j: (i, 0)),
        pl.BlockSpec((1, D_MODEL), lambda i, j: (0, 0)),
        pl.BlockSpec((D_MODEL, FFN_TF), lambda i, j: (0, j)),
        pl.BlockSpec((D_MODEL, FFN_TF), lambda i, j: (0, j)),
        pl.BlockSpec((FFN_TF, D_MODEL), lambda i, j: (j, 0)),
    ]
    out_specs = [pl.BlockSpec((FFN_TM, D_MODEL), lambda i, j: (i, 0))]
    out_shape = [jax.ShapeDtypeStruct((t, D_MODEL), F32)]
    args = [x2d, gain, wg, wu, wd]
    if prepare is not None:
        pg, pu, pd, pwin = prepare
        rows = D_MODEL // n_i
        assert D_MODEL % n_i == 0 and rows % LANES == 0
        assert pg.shape == pu.shape == (D_MODEL, D_FF) and pd.shape == (D_FF, D_MODEL)
        assert pwin.shape == (D_MODEL, D_MAIN + GLA_GATE_RANK)
        n_win = D_MODEL // W_IN_ROWS
        assert n_win <= n_i * n_j
        win_idx = lambda i, j: (jnp.minimum(i * n_j + j, n_win - 1), 0)
        up_specs = [pl.BlockSpec((rows, FFN_TF), lambda i, j: (i, j))] * 2
        down_spec = pl.BlockSpec((FFN_TF, rows), lambda i, j: (j, i))
        in_specs += up_specs + [down_spec, pl.BlockSpec((W_IN_ROWS, pwin.shape[1]), win_idx)]
        out_specs += up_specs + [down_spec, pl.BlockSpec((W_IN_ROWS, D_MAIN), win_idx),
                                 pl.BlockSpec((W_IN_ROWS, LANES), win_idx)]
        out_shape += [jax.ShapeDtypeStruct(pg.shape, BF16), jax.ShapeDtypeStruct(pu.shape, BF16),
                      jax.ShapeDtypeStruct(pd.shape, BF16),
                      jax.ShapeDtypeStruct((D_MODEL, D_MAIN), BF16),
                      jax.ShapeDtypeStruct((D_MODEL, LANES), BF16)]
        args += [pg, pu, pd, pwin]
    outs = pl.pallas_call(
        functools.partial(_ffn_body, hosted=prepare is not None),
        grid=(n_i, n_j),
        in_specs=in_specs,
        out_specs=out_specs,
        out_shape=out_shape,
        scratch_shapes=[pltpu.VMEM((FFN_TM, D_MODEL), BF16),
                        pltpu.VMEM((FFN_TM, LANES), F32)],
        compiler_params=_params(("arbitrary" if prepare is not None else "parallel", "arbitrary")),
        name="ffn",
    )(*args)
    return outs if prepare is not None else outs[0]


PROJ_TM = 512
PROJ_PIECE = 256
LOG2E = 1.4426950408889634
ATT_Q_SCALE = ATT_HD ** -0.5 * LOG2E


def _log_sigmoid(x):
    return jnp.minimum(x, 0.0) - jnp.log(1.0 + jnp.exp(-jnp.abs(x)))


def _rope_heads(z, gain, cos, sin):
    outs = []
    for hd in range(z.shape[1] // ATT_HD):
        zh = z[:, hd * ATT_HD:(hd + 1) * ATT_HD]
        zh = zh * _rms_scale(zh) * gain
        outs.append(zh * cos + pltpu.roll(zh, ATT_HD // 2, 1) * sin)
    return jnp.concatenate(outs, axis=1)


def _proj_body(x_ref, gain_ref, w_ref, wlow_ref, gup_ref, gbias_ref, qn_ref, kn_ref,
               cos_ref, sin_ref, z_ref, la_ref, h_ref, rs_ref):
    j = pl.program_id(1)

    def prologue():
        x = x_ref[...]
        h_ref[...] = (x * gain_ref[...]).astype(BF16)
        rs = jnp.broadcast_to(_rms_scale(x), rs_ref.shape)
        rs_ref[...] = rs
        g_low = _dot(h_ref[...], wlow_ref[...])
        logit = (_dot(g_low.astype(BF16), gup_ref[...]) * jnp.tile(rs, (1, GLA_QK // LANES))
                 + gbias_ref[...])
        la_ref[...] = _log_sigmoid(logit) * (1.0 / GLA_GATE_TAU)

    def piece(c0, gain):
        cols = slice(c0, c0 + PROJ_PIECE)
        z = _dot(h_ref[...], w_ref[:, cols]) * jnp.tile(rs_ref[...], (1, PROJ_PIECE // LANES))
        if gain is not None:
            z = _rope_heads(z, gain, cos_ref[...], sin_ref[...])
        z_ref[:, cols] = z.astype(z_ref.dtype)

    def has_epilogue(col):
        return COL_QA <= col < COL_VA

    def piece_gain(col):
        if not has_epilogue(col):
            return None
        return qn_ref[...] * ATT_Q_SCALE if col < COL_KA else kn_ref[...]

    for step in range(D_MAIN // PROJ_TN):
        @pl.when(j == step)
        def _(step=step):
            if step == 0:
                prologue()
            starts = list(range(0, PROJ_TN, PROJ_PIECE))
            starts.sort(key=lambda c0: not has_epilogue(step * PROJ_TN + c0))
            for c0 in starts:
                piece(c0, piece_gain(step * PROJ_TN + c0))


def _in_proj(x2d, gain, w_main, w_low, gate_up, gate_bias, q_norm, k_norm, cos, sin, seq):
    t = x2d.shape[0]
    n_seq_tiles = seq // PROJ_TM
    const = lambda i, j: (0, 0)
    return pl.pallas_call(
        _proj_body,
        grid=(t // PROJ_TM, D_MAIN // PROJ_TN),
        in_specs=[
            pl.BlockSpec((PROJ_TM, D_MODEL), lambda i, j: (i, 0)),
            pl.BlockSpec((1, D_MODEL), const),
            pl.BlockSpec((D_MODEL, PROJ_TN), lambda i, j: (0, j)),
            pl.BlockSpec((D_MODEL, LANES), const),
            pl.BlockSpec((LANES, GLA_QK), const),
            pl.BlockSpec((1, GLA_QK), const),
            pl.BlockSpec((1, ATT_HD), const),
            pl.BlockSpec((1, ATT_HD), const),
            pl.BlockSpec((PROJ_TM, ATT_HD), lambda i, j: (i % n_seq_tiles, 0)),
            pl.BlockSpec((PROJ_TM, ATT_HD), lambda i, j: (i % n_seq_tiles, 0)),
        ],
        out_specs=[
            pl.BlockSpec((PROJ_TM, PROJ_TN), lambda i, j: (i, j)),
            pl.BlockSpec((PROJ_TM, GLA_QK), lambda i, j: (i, 0)),
        ],
        out_shape=[
            jax.ShapeDtypeStruct((t, D_MAIN), BF16),
            jax.ShapeDtypeStruct((t, GLA_QK), F32),
        ],
        scratch_shapes=[pltpu.VMEM((PROJ_TM, D_MODEL), BF16),
                        pltpu.VMEM((PROJ_TM, LANES), F32)],
        compiler_params=_params(("parallel", "arbitrary")),
        name="in_proj",
    )(x2d, gain, w_main, w_low, gate_up, gate_bias, q_norm, k_norm, cos, sin)


GLA_LS = 512
GLA_GROUP = 256


def _gla_consts():
    i = np.arange(GLA_GROUP)
    same = (i[:, None] // GLA_CHUNK) == (i[None, :] // GLA_CHUNK)
    tri = (same & (i[None, :] <= i[:, None])).astype(np.float32)
    return jnp.asarray(tri, BF16), jnp.asarray(tri)


def _gla_body(q_ref, k_ref, v_ref, r_ref, la_ref, gain_ref, csum_ref, tri_ref, o_ref, st_ref):
    @pl.when(pl.program_id(1) == 0)
    def _():
        st_ref[...] = jnp.zeros_like(st_ref)

    heads = range(GLA_HEADS)
    chunks = [slice(c * GLA_CHUNK, (c + 1) * GLA_CHUNK) for c in range(GLA_GROUP // GLA_CHUNK)]

    def group(g, carry):
        rows = pl.ds(pl.multiple_of(g * GLA_GROUP, GLA_GROUP), GLA_GROUP)
        ck = [slice(h * GLA_DK, (h + 1) * GLA_DK) for h in heads]
        cv = [slice(h * GLA_DV, (h + 1) * GLA_DV) for h in heads]
        csum = csum_ref[...]
        tri = tri_ref[...] > 0.5

        b = []
        for h in heads:
            la = la_ref[0, rows, ck[h]]
            la_hi = la.astype(BF16)
            la_lo = (la - la_hi.astype(F32)).astype(BF16)
            b.append(_dot(csum, la_hi) + _dot(csum, la_lo))

        def chunk_row(x, row):
            return jnp.concatenate(
                [jnp.broadcast_to(x[cr.start + row:cr.start + row + 1, :], (GLA_CHUNK, x.shape[1]))
                 for cr in chunks], axis=0)

        b_ref = [chunk_row(b[h], GLA_CHUNK // 2 - 1) for h in heads]
        b_last = [chunk_row(b[h], GLA_CHUNK - 1) for h in heads]

        q = [q_ref[0, rows, ck[h]].astype(F32) * (GLA_DK ** -0.5) for h in heads]
        k = [k_ref[0, rows, ck[h]].astype(F32) for h in heads]
        v = [v_ref[0, rows, cv[h]].astype(BF16) for h in heads]
        q_intra = [(q[h] * jnp.exp(b[h] - b_ref[h])).astype(BF16) for h in heads]
        k_intra = [(k[h] * jnp.exp(b_ref[h] - b[h])).astype(BF16) for h in heads]
        q_inter = [(q[h] * jnp.exp(b[h])).astype(BF16) for h in heads]
        k_state = [(k[h] * jnp.exp(b_last[h] - b[h])).astype(BF16) for h in heads]

        scores = [jnp.where(tri, _dot_nt(q_intra[h], k_intra[h]), 0.0).astype(BF16) for h in heads]
        kv = [[_dot_tn(v[h][cr], k_state[h][cr]) for cr in chunks] for h in heads]
        o = [_dot(scores[h], v[h]) for h in heads]

        st = [st_ref[h] for h in heads]
        o_inter = [[] for _ in heads]
        for c, cr in enumerate(chunks):
            for h in heads:
                o_inter[h].append(_dot_nt(q_inter[h][cr], st[h].astype(BF16)))
                decay = jnp.exp(b_last[h][c * GLA_CHUNK:c * GLA_CHUNK + 1, :])
                st[h] = st[h] * decay + kv[h][c]

        gain = gain_ref[...]
        for h in heads:
            st_ref[h] = st[h]
            oh = o[h] + jnp.concatenate(o_inter[h], axis=0)
            oh = oh * _rms_scale(oh) * gain
            r = r_ref[0, rows, cv[h]].astype(F32)
            o_ref[0, rows, cv[h]] = (oh * (r * jax.nn.sigmoid(r))).astype(o_ref.dtype)
        return carry

    lax.fori_loop(0, GLA_LS // GLA_GROUP, group, 0)


def _gla(z3, la3, gain):
    bsz, seq, _ = z3.shape
    csum, tri = _gla_consts()
    const = lambda b, s: (0, 0)
    return pl.pallas_call(
        _gla_body,
        grid=(bsz, seq // GLA_LS),
        in_specs=[
            pl.BlockSpec((1, GLA_LS, GLA_QK), lambda b, s: (b, s, COL_QG // GLA_QK)),
            pl.BlockSpec((1, GLA_LS, GLA_QK), lambda b, s: (b, s, COL_KG // GLA_QK)),
            pl.BlockSpec((1, GLA_LS, GLA_WIDTH), lambda b, s: (b, s, COL_VG // GLA_WIDTH)),
            pl.BlockSpec((1, GLA_LS, GLA_WIDTH), lambda b, s: (b, s, COL_RG // GLA_WIDTH)),
            pl.BlockSpec((1, GLA_LS, GLA_QK), lambda b, s: (b, s, 0)),
            pl.BlockSpec((1, GLA_DV), const),
            pl.BlockSpec((GLA_GROUP, GLA_GROUP), const),
            pl.BlockSpec((GLA_GROUP, GLA_GROUP), const),
        ],
        out_specs=pl.BlockSpec((1, GLA_LS, GLA_WIDTH), lambda b, s: (b, s, 0)),
        out_shape=jax.ShapeDtypeStruct((bsz, seq, GLA_WIDTH), BF16),
        scratch_shapes=[pltpu.VMEM((GLA_HEADS, GLA_DV, GLA_DK), F32)],
        compiler_params=_params(("parallel", "arbitrary")),
        name="gla",
    )(z3, z3, z3, z3, la3, gain, csum, tri)


NEG = -1e30
ATT_MERGE_ROWS = 512
ATT_UNROLL = 8


def _att_consts():
    qi = np.arange(ATT_BLOCK)[:, None]
    kj = np.arange(2 * ATT_BLOCK)[None, :]
    first = kj <= qi
    dist = qi + ATT_BLOCK - kj
    band = (dist >= 0) & (dist <= ATT_BLOCK)
    bias = np.where(np.stack([first, band]), 0.0, NEG).astype(np.float32)
    return jnp.asarray(bias), jnp.ones((2 * ATT_BLOCK, LANES), BF16)


def _att_body(qb_ref, kb_ref, vb_ref, bias_ref, ones_ref, o_ref, qkv_ref, ob_ref, lse_ref, *, seq):
    def widen(i, carry):
        rs = pl.ds(pl.multiple_of(i * ATT_MERGE_ROWS, ATT_MERGE_ROWS), ATT_MERGE_ROWS)
        for a, src in enumerate((qb_ref, kb_ref, vb_ref)):
            qkv_ref[a, rs, :] = src[0, rs, :].astype(F32)
        return carry

    lax.fori_loop(0, seq // ATT_MERGE_ROWS, widen, 0)

    def blocks(br, r, specs):
        def rows(start, size):
            return pl.ds(start, size) if r == 1 else pl.ds(start, size, stride=r)

        ss, vs = [], []
        for q_start, k_start, n_keys, bias in specs:
            q = qkv_ref[0, rows(q_start, ATT_BLOCK), :].astype(BF16)
            kk = qkv_ref[1, rows(k_start, n_keys), :].astype(BF16)
            ss.append(_dot_nt(q, kk) + bias)
            vv = qkv_ref[2, rows(k_start, n_keys), :].astype(BF16)
            vs.append(jnp.concatenate([vv, ones_ref[:n_keys, :]], axis=1))
        ms = [jnp.max(s, axis=-1, keepdims=True) for s in ss]
        ps = [jnp.exp2(s - m).astype(BF16) for s, m in zip(ss, ms)]
        ols = [_dot(p, v) for p, v in zip(ps, vs)]
        for (q_start, _, _, _), m, ol in zip(specs, ms, ols):
            o, l = ol[:, :ATT_HD], ol[:, ATT_HD:]
            ob_ref[br, rows(q_start, ATT_BLOCK), :] = o / l
            lse_ref[br, rows(q_start, ATT_BLOCK), :] = m + jnp.log2(l)

    for br, (window, r) in enumerate(DIL_CONFIGS):
        assert window // r == ATT_BLOCK
        n_blocks = seq // (ATT_BLOCK * r)
        span = ATT_BLOCK * r
        u_c = min(r, ATT_UNROLL)
        u_n = ATT_UNROLL // u_c
        assert n_blocks >= 2 and r % u_c == 0 and n_blocks % u_n == 0
        c_groups = r // u_c

        if u_n == 1:
            def first_row(i, carry, br=br, r=r, u_c=u_c):
                cs = [i * u_c + u for u in range(u_c)]
                blocks(br, r, [(c, c, ATT_BLOCK, bias_ref[0, :, :ATT_BLOCK]) for c in cs])
                return carry

            def later_rows(i, carry, br=br, r=r, span=span, u_c=u_c, c_groups=c_groups):
                n = 1 + i // c_groups
                cs = [(i % c_groups) * u_c + u for u in range(u_c)]
                blocks(br, r, [(c + n * span, c + (n - 1) * span, 2 * ATT_BLOCK, bias_ref[1])
                               for c in cs])
                return carry

            lax.fori_loop(0, c_groups, first_row, 0)
            lax.fori_loop(0, (n_blocks - 1) * c_groups, later_rows, 0)
        else:
            assert c_groups == 1
            def rows_iter(i, carry, br=br, r=r, span=span, u_n=u_n):
                specs = []
                for un in range(u_n):
                    n = i * u_n + un
                    if un == 0:
                        bias = bias_ref[jnp.minimum(n, 1)]
                        k_row = jnp.maximum(n - 1, 0) * span
                    else:
                        bias = bias_ref[1]
                        k_row = (n - 1) * span
                    specs += [(c + n * span, c + k_row, 2 * ATT_BLOCK, bias) for c in range(r)]
                blocks(br, r, specs)
                return carry

            lax.fori_loop(0, n_blocks // u_n, rows_iter, 0)

    def merge(i, carry):
        rs = pl.ds(pl.multiple_of(i * ATT_MERGE_ROWS, ATT_MERGE_ROWS), ATT_MERGE_ROWS)
        lses = [lse_ref[b, rs, :] for b in range(len(DIL_CONFIGS))]
        m = functools.reduce(jnp.maximum, lses)
        ws = [jnp.exp2(x - m) for x in lses]
        den = functools.reduce(lambda a, b: a + b, ws)
        num = functools.reduce(lambda a, b: a + b, [w * ob_ref[b, rs, :] for b, w in enumerate(ws)])
        o_ref[0, rs, :] = (num / den).astype(o_ref.dtype)
        return carry

    lax.fori_loop(0, seq // ATT_MERGE_ROWS, merge, 0)


def _dil_attn(z3):
    bsz, seq, _ = z3.shape
    bias, ones = _att_consts()
    nb = len(DIL_CONFIGS)
    return pl.pallas_call(
        functools.partial(_att_body, seq=seq),
        grid=(bsz, ATT_HEADS),
        in_specs=[
            pl.BlockSpec((1, seq, ATT_HD), lambda b, h: (b, 0, COL_QA // ATT_HD + h)),
            pl.BlockSpec((1, seq, ATT_HD), lambda b, h: (b, 0, COL_KA // ATT_HD + h)),
            pl.BlockSpec((1, seq, ATT_HD), lambda b, h: (b, 0, COL_VA // ATT_HD + h)),
            pl.BlockSpec((2, ATT_BLOCK, 2 * ATT_BLOCK), lambda b, h: (0, 0, 0)),
            pl.BlockSpec((2 * ATT_BLOCK, LANES), lambda b, h: (0, 0)),
        ],
        out_specs=pl.BlockSpec((1, seq, ATT_HD), lambda b, h: (b, 0, h)),
        out_shape=jax.ShapeDtypeStruct((bsz, seq, ATT_WIDTH), BF16),
        scratch_shapes=[pltpu.VMEM((3, seq, ATT_HD), F32),
                        pltpu.VMEM((nb, seq, ATT_HD), F32),
                        pltpu.VMEM((nb, seq, ATT_HD), F32)],
        compiler_params=_params(("parallel", "parallel")),
        name="dil_attn",
    )(z3, z3, z3, bias, ones)


OUT_TM = 512
OUT_TN = D_MODEL


def _out_body(x_ref, og_ref, oa_ref, wg_ref, wa_ref, o_ref):
    o_ref[...] = x_ref[...] + _dot(og_ref[...], wg_ref[...]) + _dot(oa_ref[...], wa_ref[...])


def _out_proj(x2d, og, oa, w_top, w_bot):
    t = x2d.shape[0]
    return pl.pallas_call(
        _out_body,
        grid=(t // OUT_TM, D_MODEL // OUT_TN),
        in_specs=[
            pl.BlockSpec((OUT_TM, OUT_TN), lambda i, j: (i, j)),
            pl.BlockSpec((OUT_TM, GLA_WIDTH), lambda i, j: (i, 0)),
            pl.BlockSpec((OUT_TM, ATT_WIDTH), lambda i, j: (i, 0)),
            pl.BlockSpec((GLA_WIDTH, OUT_TN), lambda i, j: (0, j)),
            pl.BlockSpec((ATT_WIDTH, OUT_TN), lambda i, j: (0, j)),
        ],
        out_specs=pl.BlockSpec((OUT_TM, OUT_TN), lambda i, j: (i, j)),
        out_shape=jax.ShapeDtypeStruct((t, D_MODEL), F32),
        compiler_params=_params(("parallel", "arbitrary")),
        name="out_proj",
    )(x2d, og, oa, w_top, w_bot)


def _rope_tables(seq):
    half = ATT_HD // 2
    inv_freq = 1.0 / (ROPE_THETA ** (jnp.arange(half, dtype=F32) / half))
    ang = jnp.arange(seq, dtype=F32)[:, None] * inv_freq[None, :]
    cos, sin = jnp.cos(ang), jnp.sin(ang)
    return jnp.concatenate([cos, cos], axis=1), jnp.concatenate([-sin, sin], axis=1)


def kernel(x, ffn1_norm, ffn1_w_gate, ffn1_w_up, ffn1_w_down, mix_norm, w_in, gla_gate_up, gla_gate_bias, gla_out_norm, att_q_norm, att_k_norm, w_out, ffn2_norm, ffn2_w_gate, ffn2_w_up, ffn2_w_down):
    bsz, seq, d = x.shape
    depth = ffn1_norm.shape[0]
    t = bsz * seq
    cos, sin = _rope_tables(seq)
    x2d = x.reshape(t, d)
    for l in range(depth):
        x2d, w2_gate, w2_up, w2_down, w_main, w_low = _ffn(
            x2d, ffn1_norm[l][None], ffn1_w_gate[l].astype(BF16), ffn1_w_up[l].astype(BF16),
            ffn1_w_down[l].astype(BF16),
            prepare=(ffn2_w_gate[l], ffn2_w_up[l], ffn2_w_down[l], w_in[l]))

        gate_up = jnp.pad(gla_gate_up[l], ((0, LANES - GLA_GATE_RANK), (0, 0))).astype(BF16)
        z, la = _in_proj(x2d, mix_norm[l][None], w_main, w_low, gate_up, gla_gate_bias[l][None],
                         att_q_norm[l][None], att_k_norm[l][None], cos, sin, seq)
        z3 = z.reshape(bsz, seq, D_MAIN)
        o_g = _gla(z3, la.reshape(bsz, seq, GLA_QK), gla_out_norm[l][None])
        o_a = _dil_attn(z3)

        w_o = w_out[l].astype(BF16)
        x2d = _out_proj(x2d, o_g.reshape(t, GLA_WIDTH), o_a.reshape(t, ATT_WIDTH),
                        w_o[:GLA_WIDTH], w_o[GLA_WIDTH:])

        x2d = _ffn(x2d, ffn2_norm[l][None], w2_gate, w2_up, w2_down)
    return x2d.reshape(bsz, seq, d)
```

```python
import functools

import numpy as np
import jax
import jax.numpy as jnp
from jax import lax
from jax.experimental import pallas as pl
from jax.experimental.pallas import tpu as pltpu

F32 = jnp.float32
BF16 = jnp.bfloat16

D_MODEL = 2048
D_FF = 5632
EPS = 1e-6
FFN_RESIDUAL_WEIGHT = 0.5

GLA_HEADS = 4
GLA_DK = 128
GLA_DV = 256
GLA_QK = GLA_HEADS * GLA_DK
GLA_WIDTH = GLA_HEADS * GLA_DV
GLA_GATE_RANK = 16
GLA_GATE_TAU = 16.0
GLA_CHUNK = 64

ATT_HD = 128
ATT_HEADS = 8
ATT_WIDTH = ATT_HEADS * ATT_HD
DIL_CONFIGS = ((128, 1), (512, 4), (2048, 16))
ATT_BLOCK = 128
ROPE_THETA = 10000.0

VMEM_LIMIT_BYTES = 60 * 1024 * 1024
LANES = 128

COL_QG, COL_KG, COL_VG, COL_RG = 0, 512, 1024, 2048
COL_QA, COL_KA, COL_VA = 3072, 4096, 5120
D_MAIN = 6144
PROJ_TN = 2048


def _params(sem):
    return pltpu.CompilerParams(dimension_semantics=sem, vmem_limit_bytes=VMEM_LIMIT_BYTES)


def _rms_scale(x):
    return lax.rsqrt(jnp.mean(x * x, axis=-1, keepdims=True) + EPS)


def _dot(a, b):
    return jnp.dot(a, b, preferred_element_type=F32)


def _dot_nt(a, b):
    return lax.dot_general(a, b, (((1,), (1,)), ((), ())), preferred_element_type=F32)


def _dot_tn(a, b):
    return lax.dot_general(a, b, (((0,), (0,)), ((), ())), preferred_element_type=F32)


FFN_TM = 1024
FFN_TF = 512


W_IN_ROWS = 16


def _ffn_body(*refs, hosted):
    x_ref, gain_ref, wg_ref, wu_ref, wd_ref = refs[:5]
    if hosted:
        (pg_ref, pu_ref, pd_ref, pwin_ref, pwout_ref,
         o_ref, og_ref, ou_ref, od_ref, omain_ref, olow_ref, owout_ref) = refs[5:17]
    else:
        o_ref = refs[5]
    h_ref, rs_ref = refs[-2:]
    j = pl.program_id(1)

    def prepare_weights():
        for src, dst in ((pg_ref, og_ref), (pu_ref, ou_ref), (pd_ref, od_ref), (pwout_ref, owout_ref)):
            dst[...] = src[...].astype(dst.dtype)
        w = pwin_ref[...]
        omain_ref[:, :COL_RG] = w[:, :COL_RG].astype(BF16)
        omain_ref[:, COL_RG:] = w[:, COL_RG + GLA_GATE_RANK:].astype(BF16)
        low = w[:, COL_RG:COL_RG + LANES]
        lane = lax.broadcasted_iota(jnp.int32, low.shape, 1)
        olow_ref[...] = jnp.where(lane < GLA_GATE_RANK, low, 0.0).astype(BF16)

    def swiglu_down():
        h = h_ref[...]
        rs = jnp.tile(rs_ref[...], (1, FFN_TF // LANES))
        g = _dot(h, wg_ref[...]) * rs
        u = _dot(h, wu_ref[...]) * rs
        a = (g * jax.nn.sigmoid(g) * u * FFN_RESIDUAL_WEIGHT).astype(BF16)
        if hosted:
            prepare_weights()
        return _dot(a, wd_ref[...])

    @pl.when(j == 0)
    def _():
        x = x_ref[...]
        h_ref[...] = (x * gain_ref[...]).astype(BF16)
        rs_ref[...] = jnp.broadcast_to(_rms_scale(x), rs_ref.shape)
        o_ref[...] = x + swiglu_down()

    @pl.when(j > 0)
    def _():
        o_ref[...] += swiglu_down()


def _ffn(x2d, gain, wg, wu, wd, prepare=None):
    t = x2d.shape[0]
    n_i, n_j = t // FFN_TM, D_FF // FFN_TF
    in_specs = [
        pl.BlockSpec((FFN_TM, D_MODEL), lambda i, j: (i, 0)),
        pl.BlockSpec((1, D_MODEL), lambda i, j: (0, 0)),
        pl.BlockSpec((D_MODEL, FFN_TF), lambda i, j: (0, j)),
        pl.BlockSpec((D_MODEL, FFN_TF), lambda i, j: (0, j)),
        pl.BlockSpec((FFN_TF, D_MODEL), lambda i, j: (j, 0)),
    ]
    out_specs = [pl.BlockSpec((FFN_TM, D_MODEL), lambda i, j: (i, 0))]
    out_shape = [jax.ShapeDtypeStruct((t, D_MODEL), F32)]
    args = [x2d, gain, wg, wu, wd]
    if prepare is not None:
        pg, pu, pd, pwin, pwout = prepare
        rows = D_MODEL // n_i
        assert D_MODEL % n_i == 0 and rows % LANES == 0
        assert pg.shape == pu.shape == (D_MODEL, D_FF) and pd.shape == (D_FF, D_MODEL)
        assert pwin.shape == (D_MODEL, D_MAIN + GLA_GATE_RANK) and pwout.shape == (D_MODEL, D_MODEL)
        n_win = D_MODEL // W_IN_ROWS
        assert n_win <= n_i * n_j
        win_idx = lambda i, j: (jnp.minimum(i * n_j + j, n_win - 1), 0)
        up_specs = [pl.BlockSpec((rows, FFN_TF), lambda i, j: (i, j))] * 2
        down_spec = pl.BlockSpec((FFN_TF, rows), lambda i, j: (j, i))
        wout_spec = pl.BlockSpec((W_IN_ROWS, D_MODEL), win_idx)
        in_specs += up_specs + [down_spec, pl.BlockSpec((W_IN_ROWS, pwin.shape[1]), win_idx), wout_spec]
        out_specs += up_specs + [down_spec, pl.BlockSpec((W_IN_ROWS, D_MAIN), win_idx),
                                 pl.BlockSpec((W_IN_ROWS, LANES), win_idx), wout_spec]
        out_shape += [jax.ShapeDtypeStruct(pg.shape, BF16), jax.ShapeDtypeStruct(pu.shape, BF16),
                      jax.ShapeDtypeStruct(pd.shape, BF16),
                      jax.ShapeDtypeStruct((D_MODEL, D_MAIN), BF16),
                      jax.ShapeDtypeStruct((D_MODEL, LANES), BF16),
                      jax.ShapeDtypeStruct(pwout.shape, BF16)]
        args += [pg, pu, pd, pwin, pwout]
    outs = pl.pallas_call(
        functools.partial(_ffn_body, hosted=prepare is not None),
        grid=(n_i, n_j),
        in_specs=in_specs,
        out_specs=out_specs,
        out_shape=out_shape,
        scratch_shapes=[pltpu.VMEM((FFN_TM, D_MODEL), BF16),
                        pltpu.VMEM((FFN_TM, LANES), F32)],
        compiler_params=_params(("arbitrary" if prepare is not None else "parallel", "arbitrary")),
        name="ffn",
    )(*args)
    return outs if prepare is not None else outs[0]


PROJ_TM = 512
PROJ_PIECE = 256
LOG2E = 1.4426950408889634
ATT_Q_SCALE = ATT_HD ** -0.5 * LOG2E


def _log_sigmoid(x):
    return jnp.minimum(x, 0.0) - jnp.log(1.0 + jnp.exp(-jnp.abs(x)))


def _rope_heads(z, gain, cos, sin):
    outs = []
    for hd in range(z.shape[1] // ATT_HD):
        zh = z[:, hd * ATT_HD:(hd + 1) * ATT_HD]
        zh = zh * _rms_scale(zh) * gain
        outs.append(zh * cos + pltpu.roll(zh, ATT_HD // 2, 1) * sin)
    return jnp.concatenate(outs, axis=1)


def _proj_body(x_ref, gain_ref, w_ref, wlow_ref, gup_ref, gbias_ref, qn_ref, kn_ref,
               cos_ref, sin_ref, z_ref, la_ref, h_ref, rs_ref):
    step_id = pl.program_id(0)

    def normalise():
        x = x_ref[...]
        h_ref[...] = (x * gain_ref[...]).astype(BF16)
        rs = jnp.broadcast_to(_rms_scale(x), rs_ref.shape)
        rs_ref[...] = rs
        return rs

    def forget_gate(rs):
        g_low = _dot(h_ref[...], wlow_ref[...])
        logit = (_dot(g_low.astype(BF16), gup_ref[...]) * jnp.tile(rs, (1, GLA_QK // LANES))
                 + gbias_ref[...])
        la_ref[...] = _log_sigmoid(logit) * (1.0 / GLA_GATE_TAU)

    def piece(c0, gain):
        cols = slice(c0, c0 + PROJ_PIECE)
        z = _dot(h_ref[...], w_ref[:, cols]) * jnp.tile(rs_ref[...], (1, PROJ_PIECE // LANES))
        if gain is not None:
            z = _rope_heads(z, gain, cos_ref[...], sin_ref[...])
        z_ref[:, cols] = z.astype(z_ref.dtype)

    def has_epilogue(col):
        return COL_QA <= col < COL_VA

    def piece_gain(col):
        if not has_epilogue(col):
            return None
        return qn_ref[...] * ATT_Q_SCALE if col < COL_KA else kn_ref[...]

    for step in range(D_MAIN // PROJ_TN):
        @pl.when(step_id == step)
        def _(step=step):
            rs = normalise()
            if step == 0:
                forget_gate(rs)
            starts = list(range(0, PROJ_TN, PROJ_PIECE))
            starts.sort(key=lambda c0: not has_epilogue(step * PROJ_TN + c0))
            for c0 in starts:
                piece(c0, piece_gain(step * PROJ_TN + c0))


def _in_proj(x2d, gain, w_main, w_low, gate_up, gate_bias, q_norm, k_norm, cos, sin, seq):
    t = x2d.shape[0]
    n_tiles = t // PROJ_TM
    n_seq_tiles = seq // PROJ_TM
    const = lambda j, i: (0, 0)
    return pl.pallas_call(
        _proj_body,
        grid=(D_MAIN // PROJ_TN, n_tiles),
        in_specs=[
            pl.BlockSpec((PROJ_TM, D_MODEL), lambda j, i: (i, 0)),
            pl.BlockSpec((1, D_MODEL), const),
            pl.BlockSpec((D_MODEL, PROJ_TN), lambda j, i: (0, j)),
            pl.BlockSpec((D_MODEL, LANES), const),
            pl.BlockSpec((LANES, GLA_QK), const),
            pl.BlockSpec((1, GLA_QK), const),
            pl.BlockSpec((1, ATT_HD), const),
            pl.BlockSpec((1, ATT_HD), const),
            pl.BlockSpec((PROJ_TM, ATT_HD), lambda j, i: (i % n_seq_tiles, 0)),
            pl.BlockSpec((PROJ_TM, ATT_HD), lambda j, i: (i % n_seq_tiles, 0)),
        ],
        out_specs=[
            pl.BlockSpec((PROJ_TM, PROJ_TN), lambda j, i: (i, j)),
            pl.BlockSpec((PROJ_TM, GLA_QK), lambda j, i: (jnp.where(j == 0, i, n_tiles - 1), 0)),
        ],
        out_shape=[
            jax.ShapeDtypeStruct((t, D_MAIN), BF16),
            jax.ShapeDtypeStruct((t, GLA_QK), F32),
        ],
        scratch_shapes=[pltpu.VMEM((PROJ_TM, D_MODEL), BF16),
                        pltpu.VMEM((PROJ_TM, LANES), F32)],
        compiler_params=_params(("arbitrary", "arbitrary")),
        name="in_proj",
    )(x2d, gain, w_main, w_low, gate_up, gate_bias, q_norm, k_norm, cos, sin)


GLA_LS = 512
GLA_GROUP = 256


def _gla_consts():
    i = np.arange(GLA_GROUP)
    same = (i[:, None] // GLA_CHUNK) == (i[None, :] // GLA_CHUNK)
    tri = (same & (i[None, :] <= i[:, None])).astype(np.float32)
    return jnp.asarray(tri, BF16), jnp.asarray(tri)


def _gla_body(q_ref, k_ref, v_ref, r_ref, la_ref, gain_ref, csum_ref, tri_ref, o_ref, st_ref):
    @pl.when(pl.program_id(1) == 0)
    def _():
        st_ref[...] = jnp.zeros_like(st_ref)

    heads = range(GLA_HEADS)
    chunks = [slice(c * GLA_CHUNK, (c + 1) * GLA_CHUNK) for c in range(GLA_GROUP // GLA_CHUNK)]

    def group(g, carry):
        rows = pl.ds(pl.multiple_of(g * GLA_GROUP, GLA_GROUP), GLA_GROUP)
        ck = [slice(h * GLA_DK, (h + 1) * GLA_DK) for h in heads]
        cv = [slice(h * GLA_DV, (h + 1) * GLA_DV) for h in heads]
        csum = csum_ref[...]
        tri = tri_ref[...] > 0.5

        b = []
        for h in heads:
            la = la_ref[0, rows, ck[h]]
            la_hi = la.astype(BF16)
            la_lo = (la - la_hi.astype(F32)).astype(BF16)
            b.append(_dot(csum, la_hi) + _dot(csum, la_lo))

        def chunk_row(x, row):
            return jnp.concatenate(
                [jnp.broadcast_to(x[cr.start + row:cr.start + row + 1, :], (GLA_CHUNK, x.shape[1]))
                 for cr in chunks], axis=0)

        b_ref = [chunk_row(b[h], GLA_CHUNK // 2 - 1) for h in heads]
        b_last = [chunk_row(b[h], GLA_CHUNK - 1) for h in heads]

        q = [q_ref[0, rows, ck[h]].astype(F32) * (GLA_DK ** -0.5) for h in heads]
        k = [k_ref[0, rows, ck[h]].astype(F32) for h in heads]
        v = [v_ref[0, rows, cv[h]].astype(BF16) for h in heads]
        q_intra = [(q[h] * jnp.exp(b[h] - b_ref[h])).astype(BF16) for h in heads]
        k_intra = [(k[h] * jnp.exp(b_ref[h] - b[h])).astype(BF16) for h in heads]
        q_inter = [(q[h] * jnp.exp(b[h])).astype(BF16) for h in heads]
        k_state = [(k[h] * jnp.exp(b_last[h] - b[h])).astype(BF16) for h in heads]

        scores = [jnp.where(tri, _dot_nt(q_intra[h], k_intra[h]), 0.0).astype(BF16) for h in heads]
        kv = [[_dot_tn(v[h][cr], k_state[h][cr]) for cr in chunks] for h in heads]
        o = [_dot(scores[h], v[h]) for h in heads]

        st = [st_ref[h] for h in heads]
        o_inter = [[] for _ in heads]
        for c, cr in enumerate(chunks):
            for h in heads:
                o_inter[h].append(_dot_nt(q_inter[h][cr], st[h].astype(BF16)))
                decay = jnp.exp(b_last[h][c * GLA_CHUNK:c * GLA_CHUNK + 1, :])
                st[h] = st[h] * decay + kv[h][c]

        gain = gain_ref[...]
        for h in heads:
            st_ref[h] = st[h]
            oh = o[h] + jnp.concatenate(o_inter[h], axis=0)
            oh = oh * _rms_scale(oh) * gain
            r = r_ref[0, rows, cv[h]].astype(F32)
            o_ref[0, rows, cv[h]] = (oh * (r * jax.nn.sigmoid(r))).astype(o_ref.dtype)
        return carry

    lax.fori_loop(0, GLA_LS // GLA_GROUP, group, 0)


def _gla(z3, la3, gain):
    bsz, seq, _ = z3.shape
    csum, tri = _gla_consts()
    const = lambda b, s: (0, 0)
    return pl.pallas_call(
        _gla_body,
        grid=(bsz, seq // GLA_LS),
        in_specs=[
            pl.BlockSpec((1, GLA_LS, GLA_QK), lambda b, s: (b, s, COL_QG // GLA_QK)),
            pl.BlockSpec((1, GLA_LS, GLA_QK), lambda b, s: (b, s, COL_KG // GLA_QK)),
            pl.BlockSpec((1, GLA_LS, GLA_WIDTH), lambda b, s: (b, s, COL_VG // GLA_WIDTH)),
            pl.BlockSpec((1, GLA_LS, GLA_WIDTH), lambda b, s: (b, s, COL_RG // GLA_WIDTH)),
            pl.BlockSpec((1, GLA_LS, GLA_QK), lambda b, s: (b, s, 0)),
            pl.BlockSpec((1, GLA_DV), const),
            pl.BlockSpec((GLA_GROUP, GLA_GROUP), const),
            pl.BlockSpec((GLA_GROUP, GLA_GROUP), const),
        ],
        out_specs=pl.BlockSpec((1, GLA_LS, GLA_WIDTH), lambda b, s: (b, s, 0)),
        out_shape=jax.ShapeDtypeStruct((bsz, seq, GLA_WIDTH), BF16),
        scratch_shapes=[pltpu.VMEM((GLA_HEADS, GLA_DV, GLA_DK), F32)],
        compiler_params=_params(("parallel", "arbitrary")),
        name="gla",
    )(z3, z3, z3, z3, la3, gain, csum, tri)


NEG = -1e30
ATT_MERGE_ROWS = 512
ATT_UNROLL = 8


def _att_consts():
    qi = np.arange(ATT_BLOCK)[:, None]
    kj = np.arange(2 * ATT_BLOCK)[None, :]
    first = kj <= qi
    dist = qi + ATT_BLOCK - kj
    band = (dist >= 0) & (dist <= ATT_BLOCK)
    bias = np.where(np.stack([first, band]), 0.0, NEG).astype(np.float32)
    return jnp.asarray(bias), jnp.ones((2 * ATT_BLOCK, LANES), BF16)


def _att_body(qb_ref, kb_ref, vb_ref, bias_ref, ones_ref, o_ref, qkv_ref, ob_ref, lse_ref, *, seq):
    def widen(i, carry):
        rs = pl.ds(pl.multiple_of(i * ATT_MERGE_ROWS, ATT_MERGE_ROWS), ATT_MERGE_ROWS)
        for a, src in enumerate((qb_ref, kb_ref, vb_ref)):
            qkv_ref[a, rs, :] = src[0, rs, :].astype(F32)
        return carry

    lax.fori_loop(0, seq // ATT_MERGE_ROWS, widen, 0)

    def blocks(br, r, specs):
        def rows(start, size):
            return pl.ds(start, size) if r == 1 else pl.ds(start, size, stride=r)

        ss, vs = [], []
        for q_start, k_start, n_keys, bias in specs:
            q = qkv_ref[0, rows(q_start, ATT_BLOCK), :].astype(BF16)
            kk = qkv_ref[1, rows(k_start, n_keys), :].astype(BF16)
            ss.append(_dot_nt(q, kk) + bias)
            vv = qkv_ref[2, rows(k_start, n_keys), :].astype(BF16)
            vs.append(jnp.concatenate([vv, ones_ref[:n_keys, :]], axis=1))
        ms = [jnp.max(s, axis=-1, keepdims=True) for s in ss]
        ps = [jnp.exp2(s - m).astype(BF16) for s, m in zip(ss, ms)]
        ols = [_dot(p, v) for p, v in zip(ps, vs)]
        for (q_start, _, _, _), m, ol in zip(specs, ms, ols):
            o, l = ol[:, :ATT_HD], ol[:, ATT_HD:]
            ob_ref[br, rows(q_start, ATT_BLOCK), :] = o / l
            lse_ref[br, rows(q_start, ATT_BLOCK), :] = m + jnp.log2(l)

    for br, (window, r) in enumerate(DIL_CONFIGS):
        assert window // r == ATT_BLOCK
        n_blocks = seq // (ATT_BLOCK * r)
        span = ATT_BLOCK * r
        u_c = min(r, ATT_UNROLL)
        u_n = ATT_UNROLL // u_c
        assert n_blocks >= 2 and r % u_c == 0 and n_blocks % u_n == 0
        c_groups = r // u_c

        if u_n == 1:
            def first_row(i, carry, br=br, r=r, u_c=u_c):
                cs = [i * u_c + u for u in range(u_c)]
                blocks(br, r, [(c, c, ATT_BLOCK, bias_ref[0, :, :ATT_BLOCK]) for c in cs])
                return carry

            def later_rows(i, carry, br=br, r=r, span=span, u_c=u_c, c_groups=c_groups):
                n = 1 + i // c_groups
                cs = [(i % c_groups) * u_c + u for u in range(u_c)]
                blocks(br, r, [(c + n * span, c + (n - 1) * span, 2 * ATT_BLOCK, bias_ref[1])
                               for c in cs])
                return carry

            lax.fori_loop(0, c_groups, first_row, 0)
            lax.fori_loop(0, (n_blocks - 1) * c_groups, later_rows, 0)
        else:
            assert c_groups == 1
            def rows_iter(i, carry, br=br, r=r, span=span, u_n=u_n):
                specs = []
                for un in range(u_n):
                    n = i * u_n + un
                    if un == 0:
                        bias = bias_ref[jnp.minimum(n, 1)]
                        k_row = jnp.maximum(n - 1, 0) * span
                    else:
                        bias = bias_ref[1]
                        k_row = (n - 1) * span
                    specs += [(c + n * span, c + k_row, 2 * ATT_BLOCK, bias) for c in range(r)]
                blocks(br, r, specs)
                return carry

            lax.fori_loop(0, n_blocks // u_n, rows_iter, 0)

    def merge(i, carry):
        rs = pl.ds(pl.multiple_of(i * ATT_MERGE_ROWS, ATT_MERGE_ROWS), ATT_MERGE_ROWS)
        lses = [lse_ref[b, rs, :] for b in range(len(DIL_CONFIGS))]
        m = functools.reduce(jnp.maximum, lses)
        ws = [jnp.exp2(x - m) for x in lses]
        den = functools.reduce(lambda a, b: a + b, ws)
        num = functools.reduce(lambda a, b: a + b, [w * ob_ref[b, rs, :] for b, w in enumerate(ws)])
        o_ref[0, rs, :] = (num / den).astype(o_ref.dtype)
        return carry

    lax.fori_loop(0, seq // ATT_MERGE_ROWS, merge, 0)


def _dil_attn(z3):
    bsz, seq, _ = z3.shape
    bias, ones = _att_consts()
    nb = len(DIL_CONFIGS)
    return pl.pallas_call(
        functools.partial(_att_body, seq=seq),
        grid=(bsz, ATT_HEADS),
        in_specs=[
            pl.BlockSpec((1, seq, ATT_HD), lambda b, h: (b, 0, COL_QA // ATT_HD + h)),
            pl.BlockSpec((1, seq, ATT_HD), lambda b, h: (b, 0, COL_KA // ATT_HD + h)),
            pl.BlockSpec((1, seq, ATT_HD), lambda b, h: (b, 0, COL_VA // ATT_HD + h)),
            pl.BlockSpec((2, ATT_BLOCK, 2 * ATT_BLOCK), lambda b, h: (0, 0, 0)),
            pl.BlockSpec((2 * ATT_BLOCK, LANES), lambda b, h: (0, 0)),
        ],
        out_specs=pl.BlockSpec((1, seq, ATT_HD), lambda b, h: (b, 0, h)),
        out_shape=jax.ShapeDtypeStruct((bsz, seq, ATT_WIDTH), BF16),
        scratch_shapes=[pltpu.VMEM((3, seq, ATT_HD), F32),
                        pltpu.VMEM((nb, seq, ATT_HD), F32),
                        pltpu.VMEM((nb, seq, ATT_HD), F32)],
        compiler_params=_params(("parallel", "parallel")),
        name="dil_attn",
    )(z3, z3, z3, bias, ones)


OUT_TM = 512
OUT_TN = D_MODEL


def _out_body(x_ref, og_ref, oa_ref, wg_ref, wa_ref, o_ref):
    o_ref[...] = x_ref[...] + _dot(og_ref[...], wg_ref[...]) + _dot(oa_ref[...], wa_ref[...])


def _out_proj(x2d, og, oa, w_top, w_bot):
    t = x2d.shape[0]
    return pl.pallas_call(
        _out_body,
        grid=(t // OUT_TM, D_MODEL // OUT_TN),
        in_specs=[
            pl.BlockSpec((OUT_TM, OUT_TN), lambda i, j: (i, j)),
            pl.BlockSpec((OUT_TM, GLA_WIDTH), lambda i, j: (i, 0)),
            pl.BlockSpec((OUT_TM, ATT_WIDTH), lambda i, j: (i, 0)),
            pl.BlockSpec((GLA_WIDTH, OUT_TN), lambda i, j: (0, j)),
            pl.BlockSpec((ATT_WIDTH, OUT_TN), lambda i, j: (0, j)),
        ],
        out_specs=pl.BlockSpec((OUT_TM, OUT_TN), lambda i, j: (i, j)),
        out_shape=jax.ShapeDtypeStruct((t, D_MODEL), F32),
        compiler_params=_params(("parallel", "arbitrary")),
        name="out_proj",
    )(x2d, og, oa, w_top, w_bot)


def _rope_tables(seq):
    half = ATT_HD // 2
    inv_freq = 1.0 / (ROPE_THETA ** (np.arange(half, dtype=np.float64) / half))
    ang = np.arange(seq, dtype=np.float64)[:, None] * inv_freq[None, :]
    cos, sin = np.cos(ang), np.sin(ang)
    return (jnp.asarray(np.concatenate([cos, cos], axis=1), F32),
            jnp.asarray(np.concatenate([-sin, sin], axis=1), F32))


def kernel(x, ffn1_norm, ffn1_w_gate, ffn1_w_up, ffn1_w_down, mix_norm, w_in, gla_gate_up, gla_gate_bias, gla_out_norm, att_q_norm, att_k_norm, w_out, ffn2_norm, ffn2_w_gate, ffn2_w_up, ffn2_w_down):
    bsz, seq, d = x.shape
    depth = ffn1_norm.shape[0]
    t = bsz * seq
    cos, sin = _rope_tables(seq)
    x2d = x.reshape(t, d)
    for l in range(depth):
        x2d, w2_gate, w2_up, w2_down, w_main, w_low, w_o = _ffn(
            x2d, ffn1_norm[l][None], ffn1_w_gate[l].astype(BF16), ffn1_w_up[l].astype(BF16),
            ffn1_w_down[l].astype(BF16),
            prepare=(ffn2_w_gate[l], ffn2_w_up[l], ffn2_w_down[l], w_in[l], w_out[l]))

        gate_up = jnp.pad(gla_gate_up[l], ((0, LANES - GLA_GATE_RANK), (0, 0))).astype(BF16)
        z, la = _in_proj(x2d, mix_norm[l][None], w_main, w_low, gate_up, gla_gate_bias[l][None],
                         att_q_norm[l][None], att_k_norm[l][None], cos, sin, seq)
        z3 = z.reshape(bsz, seq, D_MAIN)
        o_g = _gla(z3, la.reshape(bsz, seq, GLA_QK), gla_out_norm[l][None])
        o_a = _dil_attn(z3)

        x2d = _out_proj(x2d, o_g.reshape(t, GLA_WIDTH), o_a.reshape(t, ATT_WIDTH),
                        w_o[:GLA_WIDTH], w_o[GLA_WIDTH:])

        x2d = _ffn(x2d, ffn2_norm[l][None], w2_gate, w2_up, w2_down)
    return x2d.reshape(bsz, seq, d)
```

```python
import functools

import numpy as np
import jax
import jax.numpy as jnp
from jax import lax
from jax.experimental import pallas as pl
from jax.experimental.pallas import tpu as pltpu

F32 = jnp.float32
BF16 = jnp.bfloat16

D_MODEL = 2048
D_FF = 5632
EPS = 1e-6
FFN_RESIDUAL_WEIGHT = 0.5

GLA_HEADS = 4
GLA_DK = 128
GLA_DV = 256
GLA_QK = GLA_HEADS * GLA_DK
GLA_WIDTH = GLA_HEADS * GLA_DV
GLA_GATE_RANK = 16
GLA_GATE_TAU = 16.0
GLA_CHUNK = 64

ATT_HD = 128
ATT_HEADS = 8
ATT_WIDTH = ATT_HEADS * ATT_HD
DIL_CONFIGS = ((128, 1), (512, 4), (2048, 16))
ATT_BLOCK = 128
ROPE_THETA = 10000.0

VMEM_LIMIT_BYTES = 60 * 1024 * 1024
LANES = 128

COL_QG, COL_KG, COL_VG, COL_RG = 0, 512, 1024, 2048
COL_QA, COL_KA, COL_VA = 3072, 4096, 5120
D_MAIN = 6144
PROJ_TN = 2048


def _params(sem):
    return pltpu.CompilerParams(dimension_semantics=sem, vmem_limit_bytes=VMEM_LIMIT_BYTES)


def _rms_scale(x):
    return lax.rsqrt(jnp.mean(x * x, axis=-1, keepdims=True) + EPS)


def _dot(a, b):
    return jnp.dot(a, b, preferred_element_type=F32)


def _dot_nt(a, b):
    return lax.dot_general(a, b, (((1,), (1,)), ((), ())), preferred_element_type=F32)


def _dot_tn(a, b):
    return lax.dot_general(a, b, (((0,), (0,)), ((), ())), preferred_element_type=F32)


FFN_TM = 1024
FFN_TF = 512


W_IN_ROWS = 16


def _ffn_body(*refs, hosted):
    x_ref, gain_ref, wg_ref, wu_ref, wd_ref = refs[:5]
    if hosted:
        (pg_ref, pu_ref, pd_ref, pwin_ref, pwout_ref,
         o_ref, og_ref, ou_ref, od_ref, omain_ref, olow_ref, owout_ref) = refs[5:17]
    else:
        o_ref = refs[5]
    h_ref, rs_ref = refs[-2:]
    j = pl.program_id(1)

    def prepare_weights():
        for src, dst in ((pg_ref, og_ref), (pu_ref, ou_ref), (pd_ref, od_ref), (pwout_ref, owout_ref)):
            dst[...] = src[...].astype(dst.dtype)
        w = pwin_ref[...]
        omain_ref[:, :COL_RG] = w[:, :COL_RG].astype(BF16)
        omain_ref[:, COL_RG:] = w[:, COL_RG + GLA_GATE_RANK:].astype(BF16)
        low = w[:, COL_RG:COL_RG + LANES]
        lane = lax.broadcasted_iota(jnp.int32, low.shape, 1)
        olow_ref[...] = jnp.where(lane < GLA_GATE_RANK, low, 0.0).astype(BF16)

    def swiglu_down():
        h = h_ref[...]
        rs = jnp.tile(rs_ref[...], (1, FFN_TF // LANES))
        g = _dot(h, wg_ref[...]) * rs
        u = _dot(h, wu_ref[...]) * rs
        a = (g * jax.nn.sigmoid(g) * u * FFN_RESIDUAL_WEIGHT).astype(BF16)
        if hosted:
            prepare_weights()
        return _dot(a, wd_ref[...])

    @pl.when(j == 0)
    def _():
        x = x_ref[...]
        h_ref[...] = (x * gain_ref[...]).astype(BF16)
        rs_ref[...] = jnp.broadcast_to(_rms_scale(x), rs_ref.shape)
        o_ref[...] = x + swiglu_down()

    @pl.when(j > 0)
    def _():
        o_ref[...] += swiglu_down()


def _ffn(x2d, gain, wg, wu, wd, prepare=None):
    t = x2d.shape[0]
    n_i, n_j = t // FFN_TM, D_FF // FFN_TF
    in_specs = [
        pl.BlockSpec((FFN_TM, D_MODEL), lambda i, j: (i, 0)),
        pl.BlockSpec((1, D_MODEL), lambda i, j: (0, 0)),
        pl.BlockSpec((D_MODEL, FFN_TF), lambda i, j: (0, j)),
        pl.BlockSpec((D_MODEL, FFN_TF), lambda i, j: (0, j)),
        pl.BlockSpec((FFN_TF, D_MODEL), lambda i, j: (j, 0)),
    ]
    out_specs = [pl.BlockSpec((FFN_TM, D_MODEL), lambda i, j: (i, 0))]
    out_shape = [jax.ShapeDtypeStruct((t, D_MODEL), F32)]
    args = [x2d, gain, wg, wu, wd]
    if prepare is not None:
        pg, pu, pd, pwin, pwout = prepare
        rows = D_MODEL // n_i
        assert D_MODEL % n_i == 0 and rows % LANES == 0
        assert pg.shape == pu.shape == (D_MODEL, D_FF) and pd.shape == (D_FF, D_MODEL)
        assert pwin.shape == (D_MODEL, D_MAIN + GLA_GATE_RANK) and pwout.shape == (D_MODEL, D_MODEL)
        n_win = D_MODEL // W_IN_ROWS
        assert n_win <= n_i * n_j
        win_idx = lambda i, j: (jnp.minimum(i * n_j + j, n_win - 1), 0)
        up_specs = [pl.BlockSpec((rows, FFN_TF), lambda i, j: (i, j))] * 2
        down_spec = pl.BlockSpec((FFN_TF, rows), lambda i, j: (j, i))
        wout_spec = pl.BlockSpec((W_IN_ROWS, D_MODEL), win_idx)
        in_specs += up_specs + [down_spec, pl.BlockSpec((W_IN_ROWS, pwin.shape[1]), win_idx), wout_spec]
        out_specs += up_specs + [down_spec, pl.BlockSpec((W_IN_ROWS, D_MAIN), win_idx),
                                 pl.BlockSpec((W_IN_ROWS, LANES), win_idx), wout_spec]
        out_shape += [jax.ShapeDtypeStruct(pg.shape, BF16), jax.ShapeDtypeStruct(pu.shape, BF16),
                      jax.ShapeDtypeStruct(pd.shape, BF16),
                      jax.ShapeDtypeStruct((D_MODEL, D_MAIN), BF16),
                      jax.ShapeDtypeStruct((D_MODEL, LANES), BF16),
                      jax.ShapeDtypeStruct(pwout.shape, BF16)]
        args += [pg, pu, pd, pwin, pwout]
    outs = pl.pallas_call(
        functools.partial(_ffn_body, hosted=prepare is not None),
        grid=(n_i, n_j),
        in_specs=in_specs,
        out_specs=out_specs,
        out_shape=out_shape,
        scratch_shapes=[pltpu.VMEM((FFN_TM, D_MODEL), BF16),
                        pltpu.VMEM((FFN_TM, LANES), F32)],
        compiler_params=_params(("arbitrary" if prepare is not None else "parallel", "arbitrary")),
        name="ffn",
    )(*args)
    return outs if prepare is not None else outs[0]


PROJ_TM = 512
PROJ_PIECE = 256
LOG2E = 1.4426950408889634
ATT_Q_SCALE = ATT_HD ** -0.5 * LOG2E


def _log_sigmoid(x):
    return jnp.minimum(x, 0.0) - jnp.log(1.0 + jnp.exp(-jnp.abs(x)))


def _rope_heads(z, gain, cos, sin):
    outs = []
    for hd in range(z.shape[1] // ATT_HD):
        zh = z[:, hd * ATT_HD:(hd + 1) * ATT_HD]
        zh = zh * _rms_scale(zh) * gain
        outs.append(zh * cos + pltpu.roll(zh, ATT_HD // 2, 1) * sin)
    return jnp.concatenate(outs, axis=1)


def _proj_body(x_ref, gain_ref, w_ref, wlow_ref, gup_ref, gbias_ref, qn_ref, kn_ref,
               cos_ref, sin_ref, z_ref, la_ref, h_ref, rs_ref):
    step_id = pl.program_id(0)

    def normalise():
        x = x_ref[...]
        h_ref[...] = (x * gain_ref[...]).astype(BF16)
        rs = jnp.broadcast_to(_rms_scale(x), rs_ref.shape)
        rs_ref[...] = rs
        return rs

    def forget_gate(rs):
        g_low = _dot(h_ref[...], wlow_ref[...])
        logit = (_dot(g_low.astype(BF16), gup_ref[...]) * jnp.tile(rs, (1, GLA_QK // LANES))
                 + gbias_ref[...])
        la_ref[...] = _log_sigmoid(logit) * (1.0 / GLA_GATE_TAU)

    def piece(c0, gain):
        cols = slice(c0, c0 + PROJ_PIECE)
        z = _dot(h_ref[...], w_ref[:, cols]) * jnp.tile(rs_ref[...], (1, PROJ_PIECE // LANES))
        if gain is not None:
            z = _rope_heads(z, gain, cos_ref[...], sin_ref[...])
        z_ref[:, cols] = z.astype(z_ref.dtype)

    def has_epilogue(col):
        return COL_QA <= col < COL_VA

    def piece_gain(col):
        if not has_epilogue(col):
            return None
        return qn_ref[...] * ATT_Q_SCALE if col < COL_KA else kn_ref[...]

    for step in range(D_MAIN // PROJ_TN):
        @pl.when(step_id == step)
        def _(step=step):
            rs = normalise()
            if step == 0:
                forget_gate(rs)
            starts = list(range(0, PROJ_TN, PROJ_PIECE))
            starts.sort(key=lambda c0: not has_epilogue(step * PROJ_TN + c0))
            for c0 in starts:
                piece(c0, piece_gain(step * PROJ_TN + c0))


def _in_proj(x2d, gain, w_main, w_low, gate_up, gate_bias, q_norm, k_norm, cos, sin, seq):
    t = x2d.shape[0]
    n_tiles = t // PROJ_TM
    n_seq_tiles = seq // PROJ_TM
    const = lambda j, i: (0, 0)
    return pl.pallas_call(
        _proj_body,
        grid=(D_MAIN // PROJ_TN, n_tiles),
        in_specs=[
            pl.BlockSpec((PROJ_TM, D_MODEL), lambda j, i: (i, 0)),
            pl.BlockSpec((1, D_MODEL), const),
            pl.BlockSpec((D_MODEL, PROJ_TN), lambda j, i: (0, j)),
            pl.BlockSpec((D_MODEL, LANES), const),
            pl.BlockSpec((LANES, GLA_QK), const),
            pl.BlockSpec((1, GLA_QK), const),
            pl.BlockSpec((1, ATT_HD), const),
            pl.BlockSpec((1, ATT_HD), const),
            pl.BlockSpec((PROJ_TM, ATT_HD), lambda j, i: (i % n_seq_tiles, 0)),
            pl.BlockSpec((PROJ_TM, ATT_HD), lambda j, i: (i % n_seq_tiles, 0)),
        ],
        out_specs=[
            pl.BlockSpec((PROJ_TM, PROJ_TN), lambda j, i: (i, j)),
            pl.BlockSpec((PROJ_TM, GLA_QK), lambda j, i: (jnp.where(j == 0, i, n_tiles - 1), 0)),
        ],
        out_shape=[
            jax.ShapeDtypeStruct((t, D_MAIN), BF16),
            jax.ShapeDtypeStruct((t, GLA_QK), F32),
        ],
        scratch_shapes=[pltpu.VMEM((PROJ_TM, D_MODEL), BF16),
                        pltpu.VMEM((PROJ_TM, LANES), F32)],
        compiler_params=_params(("arbitrary", "arbitrary")),
        name="in_proj",
    )(x2d, gain, w_main, w_low, gate_up, gate_bias, q_norm, k_norm, cos, sin)


GLA_LS = 1024
GLA_GROUP = 256


def _gla_consts():
    i = np.arange(GLA_GROUP)
    same = (i[:, None] // GLA_CHUNK) == (i[None, :] // GLA_CHUNK)
    tri = (same & (i[None, :] <= i[:, None])).astype(np.float32)
    return jnp.asarray(tri, BF16), jnp.asarray(tri)


def _gla_body(q_ref, k_ref, v_ref, r_ref, la_ref, gain_ref, csum_ref, tri_ref, o_ref, st_ref):
    @pl.when(pl.program_id(1) == 0)
    def _():
        st_ref[...] = jnp.zeros_like(st_ref)

    heads = range(GLA_HEADS)
    chunks = [slice(c * GLA_CHUNK, (c + 1) * GLA_CHUNK) for c in range(GLA_GROUP // GLA_CHUNK)]

    def group(g, carry):
        rows = pl.ds(pl.multiple_of(g * GLA_GROUP, GLA_GROUP), GLA_GROUP)
        ck = [slice(h * GLA_DK, (h + 1) * GLA_DK) for h in heads]
        cv = [slice(h * GLA_DV, (h + 1) * GLA_DV) for h in heads]
        csum = csum_ref[...]
        tri = tri_ref[...] > 0.5

        b = []
        for h in heads:
            la = la_ref[0, rows, ck[h]]
            la_hi = la.astype(BF16)
            la_lo = (la - la_hi.astype(F32)).astype(BF16)
            b.append(_dot(csum, la_hi) + _dot(csum, la_lo))

        def chunk_row(x, row):
            return jnp.concatenate(
                [jnp.broadcast_to(x[cr.start + row:cr.start + row + 1, :], (GLA_CHUNK, x.shape[1]))
                 for cr in chunks], axis=0)

        b_ref = [chunk_row(b[h], GLA_CHUNK // 2 - 1) for h in heads]
        b_last = [chunk_row(b[h], GLA_CHUNK - 1) for h in heads]

        q = [q_ref[0, rows, ck[h]].astype(F32) * (GLA_DK ** -0.5) for h in heads]
        k = [k_ref[0, rows, ck[h]].astype(F32) for h in heads]
        v = [v_ref[0, rows, cv[h]].astype(BF16) for h in heads]
        q_intra = [(q[h] * jnp.exp(b[h] - b_ref[h])).astype(BF16) for h in heads]
        k_intra = [(k[h] * jnp.exp(b_ref[h] - b[h])).astype(BF16) for h in heads]
        q_inter = [(q[h] * jnp.exp(b[h])).astype(BF16) for h in heads]
        k_state = [(k[h] * jnp.exp(b_last[h] - b[h])).astype(BF16) for h in heads]

        scores = [jnp.where(tri, _dot_nt(q_intra[h], k_intra[h]), 0.0).astype(BF16) for h in heads]
        kv = [[_dot_tn(v[h][cr], k_state[h][cr]) for cr in chunks] for h in heads]
        o = [_dot(scores[h], v[h]) for h in heads]

        st = [st_ref[h] for h in heads]
        o_inter = [[] for _ in heads]
        for c, cr in enumerate(chunks):
            for h in heads:
                o_inter[h].append(_dot_nt(q_inter[h][cr], st[h].astype(BF16)))
                decay = jnp.exp(b_last[h][c * GLA_CHUNK:c * GLA_CHUNK + 1, :])
                st[h] = st[h] * decay + kv[h][c]

        gain = gain_ref[...]
        for h in heads:
            st_ref[h] = st[h]
            oh = o[h] + jnp.concatenate(o_inter[h], axis=0)
            oh = oh * _rms_scale(oh) * gain
            r = r_ref[0, rows, cv[h]].astype(F32)
            o_ref[0, rows, cv[h]] = (oh * (r * jax.nn.sigmoid(r))).astype(o_ref.dtype)
        return carry

    lax.fori_loop(0, GLA_LS // GLA_GROUP, group, 0)


def _gla(z3, la3, gain):
    bsz, seq, _ = z3.shape
    csum, tri = _gla_consts()
    const = lambda b, s: (0, 0)
    return pl.pallas_call(
        _gla_body,
        grid=(bsz, seq // GLA_LS),
        in_specs=[
            pl.BlockSpec((1, GLA_LS, GLA_QK), lambda b, s: (b, s, COL_QG // GLA_QK)),
            pl.BlockSpec((1, GLA_LS, GLA_QK), lambda b, s: (b, s, COL_KG // GLA_QK)),
            pl.BlockSpec((1, GLA_LS, GLA_WIDTH), lambda b, s: (b, s, COL_VG // GLA_WIDTH)),
            pl.BlockSpec((1, GLA_LS, GLA_WIDTH), lambda b, s: (b, s, COL_RG // GLA_WIDTH)),
            pl.BlockSpec((1, GLA_LS, GLA_QK), lambda b, s: (b, s, 0)),
            pl.BlockSpec((1, GLA_DV), const),
            pl.BlockSpec((GLA_GROUP, GLA_GROUP), const),
            pl.BlockSpec((GLA_GROUP, GLA_GROUP), const),
        ],
        out_specs=pl.BlockSpec((1, GLA_LS, GLA_WIDTH), lambda b, s: (b, s, 0)),
        out_shape=jax.ShapeDtypeStruct((bsz, seq, GLA_WIDTH), BF16),
        scratch_shapes=[pltpu.VMEM((GLA_HEADS, GLA_DV, GLA_DK), F32)],
        compiler_params=_params(("parallel", "arbitrary")),
        name="gla",
    )(z3, z3, z3, z3, la3, gain, csum, tri)


NEG = -1e30
ATT_MERGE_ROWS = 512
ATT_UNROLL = 16


def _att_consts():
    qi = np.arange(ATT_BLOCK)[:, None]
    kj = np.arange(2 * ATT_BLOCK)[None, :]
    first = kj <= qi
    dist = qi + ATT_BLOCK - kj
    band = (dist >= 0) & (dist <= ATT_BLOCK)
    bias = np.where(np.stack([first, band]), 0.0, NEG).astype(np.float32)
    return jnp.asarray(bias), jnp.ones((2 * ATT_BLOCK, LANES), BF16)


def _att_body(qb_ref, kb_ref, vb_ref, bias_ref, ones_ref, o_ref, qkv_ref, ob_ref, lse_ref, *, seq):
    def widen(i, carry):
        rs = pl.ds(pl.multiple_of(i * ATT_MERGE_ROWS, ATT_MERGE_ROWS), ATT_MERGE_ROWS)
        for a, src in enumerate((qb_ref, kb_ref, vb_ref)):
            qkv_ref[a, rs, :] = src[0, rs, :].astype(F32)
        return carry

    lax.fori_loop(0, seq // ATT_MERGE_ROWS, widen, 0)

    def blocks(br, r, specs):
        def rows(start, size):
            return pl.ds(start, size) if r == 1 else pl.ds(start, size, stride=r)

        ss, vs = [], []
        for q_start, k_start, n_keys, bias in specs:
            q = qkv_ref[0, rows(q_start, ATT_BLOCK), :].astype(BF16)
            kk = qkv_ref[1, rows(k_start, n_keys), :].astype(BF16)
            ss.append(_dot_nt(q, kk) + bias)
            vv = qkv_ref[2, rows(k_start, n_keys), :].astype(BF16)
            vs.append(jnp.concatenate([vv, ones_ref[:n_keys, :]], axis=1))
        ms = [jnp.max(s, axis=-1, keepdims=True) for s in ss]
        ps = [jnp.exp2(s - m).astype(BF16) for s, m in zip(ss, ms)]
        ols = [_dot(p, v) for p, v in zip(ps, vs)]
        for (q_start, _, _, _), m, ol in zip(specs, ms, ols):
            o, l = ol[:, :ATT_HD], ol[:, ATT_HD:]
            ob_ref[br, rows(q_start, ATT_BLOCK), :] = o / l
            lse_ref[br, rows(q_start, ATT_BLOCK), :] = m + jnp.log2(l)

    for br, (window, r) in enumerate(DIL_CONFIGS):
        assert window // r == ATT_BLOCK
        n_blocks = seq // (ATT_BLOCK * r)
        span = ATT_BLOCK * r
        u_c = min(r, ATT_UNROLL)
        u_n = ATT_UNROLL // u_c
        assert n_blocks >= 2 and r % u_c == 0 and n_blocks % u_n == 0
        c_groups = r // u_c

        if u_n == 1:
            def first_row(i, carry, br=br, r=r, u_c=u_c):
                cs = [i * u_c + u for u in range(u_c)]
                blocks(br, r, [(c, c, ATT_BLOCK, bias_ref[0, :, :ATT_BLOCK]) for c in cs])
                return carry

            def later_rows(i, carry, br=br, r=r, span=span, u_c=u_c, c_groups=c_groups):
                n = 1 + i // c_groups
                cs = [(i % c_groups) * u_c + u for u in range(u_c)]
                blocks(br, r, [(c + n * span, c + (n - 1) * span, 2 * ATT_BLOCK, bias_ref[1])
                               for c in cs])
                return carry

            lax.fori_loop(0, c_groups, first_row, 0)
            lax.fori_loop(0, (n_blocks - 1) * c_groups, later_rows, 0)
        else:
            assert c_groups == 1
            def rows_iter(i, carry, br=br, r=r, span=span, u_n=u_n):
                specs = []
                for un in range(u_n):
                    n = i * u_n + un
                    if un == 0:
                        bias = bias_ref[jnp.minimum(n, 1)]
                        k_row = jnp.maximum(n - 1, 0) * span
                    else:
                        bias = bias_ref[1]
                        k_row = (n - 1) * span
                    specs += [(c + n * span, c + k_row, 2 * ATT_BLOCK, bias) for c in range(r)]
                blocks(br, r, specs)
                return carry

            lax.fori_loop(0, n_blocks // u_n, rows_iter, 0)

    def merge(i, carry):
        rs = pl.ds(pl.multiple_of(i * ATT_MERGE_ROWS, ATT_MERGE_ROWS), ATT_MERGE_ROWS)
        lses = [lse_ref[b, rs, :] for b in range(len(DIL_CONFIGS))]
        m = functools.reduce(jnp.maximum, lses)
        ws = [jnp.exp2(x - m) for x in lses]
        den = functools.reduce(lambda a, b: a + b, ws)
        num = functools.reduce(lambda a, b: a + b, [w * ob_ref[b, rs, :] for b, w in enumerate(ws)])
        o_ref[0, rs, :] = (num / den).astype(o_ref.dtype)
        return carry

    lax.fori_loop(0, seq // ATT_MERGE_ROWS, merge, 0)


def _dil_attn(z3):
    bsz, seq, _ = z3.shape
    bias, ones = _att_consts()
    nb = len(DIL_CONFIGS)
    return pl.pallas_call(
        functools.partial(_att_body, seq=seq),
        grid=(bsz, ATT_HEADS),
        in_specs=[
            pl.BlockSpec((1, seq, ATT_HD), lambda b, h: (b, 0, COL_QA // ATT_HD + h)),
            pl.BlockSpec((1, seq, ATT_HD), lambda b, h: (b, 0, COL_KA // ATT_HD + h)),
            pl.BlockSpec((1, seq, ATT_HD), lambda b, h: (b, 0, COL_VA // ATT_HD + h)),
            pl.BlockSpec((2, ATT_BLOCK, 2 * ATT_BLOCK), lambda b, h: (0, 0, 0)),
            pl.BlockSpec((2 * ATT_BLOCK, LANES), lambda b, h: (0, 0)),
        ],
        out_specs=pl.BlockSpec((1, seq, ATT_HD), lambda b, h: (b, 0, h)),
        out_shape=jax.ShapeDtypeStruct((bsz, seq, ATT_WIDTH), BF16),
        scratch_shapes=[pltpu.VMEM((3, seq, ATT_HD), F32),
                        pltpu.VMEM((nb, seq, ATT_HD), F32),
                        pltpu.VMEM((nb, seq, ATT_HD), F32)],
        compiler_params=_params(("parallel", "parallel")),
        name="dil_attn",
    )(z3, z3, z3, bias, ones)


OUT_TM = 512
OUT_TN = D_MODEL


def _out_body(x_ref, og_ref, oa_ref, wg_ref, wa_ref, o_ref):
    o_ref[...] = x_ref[...] + _dot(og_ref[...], wg_ref[...]) + _dot(oa_ref[...], wa_ref[...])


def _out_proj(x2d, og, oa, w_top, w_bot):
    t = x2d.shape[0]
    return pl.pallas_call(
        _out_body,
        grid=(t // OUT_TM, D_MODEL // OUT_TN),
        in_specs=[
            pl.BlockSpec((OUT_TM, OUT_TN), lambda i, j: (i, j)),
            pl.BlockSpec((OUT_TM, GLA_WIDTH), lambda i, j: (i, 0)),
            pl.BlockSpec((OUT_TM, ATT_WIDTH), lambda i, j: (i, 0)),
            pl.BlockSpec((GLA_WIDTH, OUT_TN), lambda i, j: (0, j)),
            pl.BlockSpec((ATT_WIDTH, OUT_TN), lambda i, j: (0, j)),
        ],
        out_specs=pl.BlockSpec((OUT_TM, OUT_TN), lambda i, j: (i, j)),
        out_shape=jax.ShapeDtypeStruct((t, D_MODEL), F32),
        compiler_params=_params(("parallel", "arbitrary")),
        name="out_proj",
    )(x2d, og, oa, w_top, w_bot)


def _rope_tables(seq):
    half = ATT_HD // 2
    inv_freq = 1.0 / (ROPE_THETA ** (np.arange(half, dtype=np.float64) / half))
    ang = np.arange(seq, dtype=np.float64)[:, None] * inv_freq[None, :]
    cos, sin = np.cos(ang), np.sin(ang)
    return (jnp.asarray(np.concatenate([cos, cos], axis=1), F32),
            jnp.asarray(np.concatenate([-sin, sin], axis=1), F32))


def kernel(x, ffn1_norm, ffn1_w_gate, ffn1_w_up, ffn1_w_down, mix_norm, w_in, gla_gate_up, gla_gate_bias, gla_out_norm, att_q_norm, att_k_norm, w_out, ffn2_norm, ffn2_w_gate, ffn2_w_up, ffn2_w_down):
    bsz, seq, d = x.shape
    depth = ffn1_norm.shape[0]
    t = bsz * seq
    cos, sin = _rope_tables(seq)
    x2d = x.reshape(t, d)
    for l in range(depth):
        x2d, w2_gate, w2_up, w2_down, w_main, w_low, w_o = _ffn(
            x2d, ffn1_norm[l][None], ffn1_w_gate[l].astype(BF16), ffn1_w_up[l].astype(BF16),
            ffn1_w_down[l].astype(BF16),
            prepare=(ffn2_w_gate[l], ffn2_w_up[l], ffn2_w_down[l], w_in[l].astype(BF16), w_out[l]))

        gate_up = jnp.pad(gla_gate_up[l], ((0, LANES - GLA_GATE_RANK), (0, 0))).astype(BF16)
        z, la = _in_proj(x2d, mix_norm[l][None], w_main, w_low, gate_up, gla_gate_bias[l][None],
                         att_q_norm[l][None], att_k_norm[l][None], cos, sin, seq)
        z3 = z.reshape(bsz, seq, D_MAIN)
        o_g = _gla(z3, la.reshape(bsz, seq, GLA_QK), gla_out_norm[l][None])
        o_a = _dil_attn(z3)

        x2d = _out_proj(x2d, o_g.reshape(t, GLA_WIDTH), o_a.reshape(t, ATT_WIDTH),
                        w_o[:GLA_WIDTH], w_o[GLA_WIDTH:])

        x2d = _ffn(x2d, ffn2_norm[l][None], w2_gate, w2_up, w2_down)
    return x2d.reshape(bsz, seq, d)
```

```python
import functools

import numpy as np
import jax
import jax.numpy as jnp
from jax import lax
from jax.experimental import pallas as pl
from jax.experimental.pallas import tpu as pltpu

F32 = jnp.float32
BF16 = jnp.bfloat16

D_MODEL = 2048
D_FF = 5632
EPS = 1e-6
FFN_RESIDUAL_WEIGHT = 0.5

GLA_HEADS = 4
GLA_DK = 128
GLA_DV = 256
GLA_QK = GLA_HEADS * GLA_DK
GLA_WIDTH = GLA_HEADS * GLA_DV
GLA_GATE_RANK = 16
GLA_GATE_TAU = 16.0
GLA_CHUNK = 64

ATT_HD = 128
ATT_HEADS = 8
ATT_WIDTH = ATT_HEADS * ATT_HD
DIL_CONFIGS = ((128, 1), (512, 4), (2048, 16))
ATT_BLOCK = 128
ROPE_THETA = 10000.0

VMEM_LIMIT_BYTES = 60 * 1024 * 1024
LANES = 128

COL_QG, COL_KG, COL_VG, COL_RG = 0, 512, 1024, 2048
COL_QA, COL_KA, COL_VA = 3072, 4096, 5120
D_MAIN = 6144
PROJ_TN = 2048


def _params(sem):
    return pltpu.CompilerParams(dimension_semantics=sem, vmem_limit_bytes=VMEM_LIMIT_BYTES)


def _rms_scale(x):
    return lax.rsqrt(jnp.mean(x * x, axis=-1, keepdims=True) + EPS)


def _dot(a, b):
    return jnp.dot(a, b, preferred_element_type=F32)


def _dot_nt(a, b):
    return lax.dot_general(a, b, (((1,), (1,)), ((), ())), preferred_element_type=F32)


def _dot_tn(a, b):
    return lax.dot_general(a, b, (((0,), (0,)), ((), ())), preferred_element_type=F32)


FFN_TM = 1024
FFN_TF = 512


W_IN_ROWS = 16


def _ffn_body(*refs, hosted):
    x_ref, gain_ref, wg_ref, wu_ref, wd_ref = refs[:5]
    if hosted:
        (pg_ref, pu_ref, pd_ref, pwin_ref, pwout_ref,
         o_ref, og_ref, ou_ref, od_ref, omain_ref, olow_ref, owout_ref) = refs[5:17]
    else:
        o_ref = refs[5]
    h_ref, rs_ref = refs[-2:]
    j = pl.program_id(1)

    def prepare_weights():
        for src, dst in ((pg_ref, og_ref), (pu_ref, ou_ref), (pd_ref, od_ref), (pwout_ref, owout_ref)):
            dst[...] = src[...].astype(dst.dtype)
        w = pwin_ref[...]
        omain_ref[:, :COL_RG] = w[:, :COL_RG].astype(BF16)
        omain_ref[:, COL_RG:] = w[:, COL_RG + GLA_GATE_RANK:].astype(BF16)
        low = w[:, COL_RG:COL_RG + LANES]
        lane = lax.broadcasted_iota(jnp.int32, low.shape, 1)
        olow_ref[...] = jnp.where(lane < GLA_GATE_RANK, low, 0.0).astype(BF16)

    def swiglu_down():
        h = h_ref[...]
        rs = jnp.tile(rs_ref[...], (1, FFN_TF // LANES))
        g = _dot(h, wg_ref[...]) * rs
        u = _dot(h, wu_ref[...]) * rs
        a = (g * jax.nn.sigmoid(g) * u * FFN_RESIDUAL_WEIGHT).astype(BF16)
        if hosted:
            prepare_weights()
        return _dot(a, wd_ref[...])

    @pl.when(j == 0)
    def _():
        x = x_ref[...]
        h_ref[...] = (x * gain_ref[...]).astype(BF16)
        rs_ref[...] = jnp.broadcast_to(_rms_scale(x), rs_ref.shape)
        o_ref[...] = x + swiglu_down()

    @pl.when(j > 0)
    def _():
        o_ref[...] += swiglu_down()


def _ffn(x2d, gain, wg, wu, wd, prepare=None):
    t = x2d.shape[0]
    n_i, n_j = t // FFN_TM, D_FF // FFN_TF
    in_specs = [
        pl.BlockSpec((FFN_TM, D_MODEL), lambda i, j: (i, 0)),
        pl.BlockSpec((1, D_MODEL), lambda i, j: (0, 0)),
        pl.BlockSpec((D_MODEL, FFN_TF), lambda i, j: (0, j)),
        pl.BlockSpec((D_MODEL, FFN_TF), lambda i, j: (0, j)),
        pl.BlockSpec((FFN_TF, D_MODEL), lambda i, j: (j, 0)),
    ]
    out_specs = [pl.BlockSpec((FFN_TM, D_MODEL), lambda i, j: (i, 0))]
    out_shape = [jax.ShapeDtypeStruct((t, D_MODEL), F32)]
    args = [x2d, gain, wg, wu, wd]
    if prepare is not None:
        pg, pu, pd, pwin, pwout = prepare
        rows = D_MODEL // n_i
        assert D_MODEL % n_i == 0 and rows % LANES == 0
        assert pg.shape == pu.shape == (D_MODEL, D_FF) and pd.shape == (D_FF, D_MODEL)
        assert pwin.shape == (D_MODEL, D_MAIN + GLA_GATE_RANK) and pwout.shape == (D_MODEL, D_MODEL)
        n_win = D_MODEL // W_IN_ROWS
        assert n_win <= n_i * n_j
        win_idx = lambda i, j: (jnp.minimum(i * n_j + j, n_win - 1), 0)
        up_specs = [pl.BlockSpec((rows, FFN_TF), lambda i, j: (i, j))] * 2
        down_spec = pl.BlockSpec((FFN_TF, rows), lambda i, j: (j, i))
        wout_spec = pl.BlockSpec((W_IN_ROWS, D_MODEL), win_idx)
        in_specs += up_specs + [down_spec, pl.BlockSpec((W_IN_ROWS, pwin.shape[1]), win_idx), wout_spec]
        out_specs += up_specs + [down_spec, pl.BlockSpec((W_IN_ROWS, D_MAIN), win_idx),
                                 pl.BlockSpec((W_IN_ROWS, LANES), win_idx), wout_spec]
        out_shape += [jax.ShapeDtypeStruct(pg.shape, BF16), jax.ShapeDtypeStruct(pu.shape, BF16),
                      jax.ShapeDtypeStruct(pd.shape, BF16),
                      jax.ShapeDtypeStruct((D_MODEL, D_MAIN), BF16),
                      jax.ShapeDtypeStruct((D_MODEL, LANES), BF16),
                      jax.ShapeDtypeStruct(pwout.shape, BF16)]
        args += [pg, pu, pd, pwin, pwout]
    outs = pl.pallas_call(
        functools.partial(_ffn_body, hosted=prepare is not None),
        grid=(n_i, n_j),
        in_specs=in_specs,
        out_specs=out_specs,
        out_shape=out_shape,
        scratch_shapes=[pltpu.VMEM((FFN_TM, D_MODEL), BF16),
                        pltpu.VMEM((FFN_TM, LANES), F32)],
        compiler_params=_params(("arbitrary" if prepare is not None else "parallel", "arbitrary")),
        name="ffn",
    )(*args)
    return outs if prepare is not None else outs[0]


PROJ_TM = 512
PROJ_PIECE = 256
LOG2E = 1.4426950408889634
ATT_Q_SCALE = ATT_HD ** -0.5 * LOG2E


def _log_sigmoid(x):
    return jnp.minimum(x, 0.0) - jnp.log(1.0 + jnp.exp(-jnp.abs(x)))


def _rope_heads(z, gain, cos, sin):
    outs = []
    for hd in range(z.shape[1] // ATT_HD):
        zh = z[:, hd * ATT_HD:(hd + 1) * ATT_HD]
        zh = zh * _rms_scale(zh) * gain
        outs.append(zh * cos + pltpu.roll(zh, ATT_HD // 2, 1) * sin)
    return jnp.concatenate(outs, axis=1)


def _proj_body(x_ref, gain_ref, w_ref, wlow_ref, gup_ref, gbias_ref, qn_ref, kn_ref,
               cos_ref, sin_ref, z_ref, la_ref, h_ref, rs_ref):
    step_id = pl.program_id(0)

    def normalise():
        x = x_ref[...]
        h_ref[...] = (x * gain_ref[...]).astype(BF16)
        rs = jnp.broadcast_to(_rms_scale(x), rs_ref.shape)
        rs_ref[...] = rs
        return rs

    def forget_gate(rs):
        g_low = _dot(h_ref[...], wlow_ref[...])
        logit = (_dot(g_low.astype(BF16), gup_ref[...]) * jnp.tile(rs, (1, GLA_QK // LANES))
                 + gbias_ref[...])
        la_ref[...] = _log_sigmoid(logit) * (1.0 / GLA_GATE_TAU)

    def piece(c0, gain):
        cols = slice(c0, c0 + PROJ_PIECE)
        z = _dot(h_ref[...], w_ref[:, cols]) * jnp.tile(rs_ref[...], (1, PROJ_PIECE // LANES))
        if gain is not None:
            z = _rope_heads(z, gain, cos_ref[...], sin_ref[...])
        z_ref[:, cols] = z.astype(z_ref.dtype)

    def has_epilogue(col):
        return COL_QA <= col < COL_VA

    def piece_gain(col):
        if not has_epilogue(col):
            return None
        return qn_ref[...] * ATT_Q_SCALE if col < COL_KA else kn_ref[...]

    for step in range(D_MAIN // PROJ_TN):
        @pl.when(step_id == step)
        def _(step=step):
            rs = normalise()
            if step == 0:
                forget_gate(rs)
            starts = list(range(0, PROJ_TN, PROJ_PIECE))
            starts.sort(key=lambda c0: not has_epilogue(step * PROJ_TN + c0))
            for c0 in starts:
                piece(c0, piece_gain(step * PROJ_TN + c0))


def _in_proj(x2d, gain, w_main, w_low, gate_up, gate_bias, q_norm, k_norm, cos, sin, seq):
    t = x2d.shape[0]
    n_tiles = t // PROJ_TM
    n_seq_tiles = seq // PROJ_TM
    const = lambda j, i: (0, 0)
    return pl.pallas_call(
        _proj_body,
        grid=(D_MAIN // PROJ_TN, n_tiles),
        in_specs=[
            pl.BlockSpec((PROJ_TM, D_MODEL), lambda j, i: (i, 0)),
            pl.BlockSpec((1, D_MODEL), const),
            pl.BlockSpec((D_MODEL, PROJ_TN), lambda j, i: (0, j)),
            pl.BlockSpec((D_MODEL, LANES), const),
            pl.BlockSpec((LANES, GLA_QK), const),
            pl.BlockSpec((1, GLA_QK), const),
            pl.BlockSpec((1, ATT_HD), const),
            pl.BlockSpec((1, ATT_HD), const),
            pl.BlockSpec((PROJ_TM, ATT_HD), lambda j, i: (i % n_seq_tiles, 0)),
            pl.BlockSpec((PROJ_TM, ATT_HD), lambda j, i: (i % n_seq_tiles, 0)),
        ],
        out_specs=[
            pl.BlockSpec((PROJ_TM, PROJ_TN), lambda j, i: (i, j)),
            pl.BlockSpec((PROJ_TM, GLA_QK), lambda j, i: (jnp.where(j == 0, i, n_tiles - 1), 0)),
        ],
        out_shape=[
            jax.ShapeDtypeStruct((t, D_MAIN), BF16),
            jax.ShapeDtypeStruct((t, GLA_QK), F32),
        ],
        scratch_shapes=[pltpu.VMEM((PROJ_TM, D_MODEL), BF16),
                        pltpu.VMEM((PROJ_TM, LANES), F32)],
        compiler_params=_params(("arbitrary", "arbitrary")),
        name="in_proj",
    )(x2d, gain, w_main, w_low, gate_up, gate_bias, q_norm, k_norm, cos, sin)


GLA_LS = 1024
GLA_GROUP = 256


def _gla_consts():
    i = np.arange(GLA_GROUP)
    same = (i[:, None] // GLA_CHUNK) == (i[None, :] // GLA_CHUNK)
    tri = (same & (i[None, :] <= i[:, None])).astype(np.float32)
    return jnp.asarray(tri, BF16), jnp.asarray(tri)


def _gla_body(q_ref, k_ref, v_ref, r_ref, la_ref, gain_ref, csum_ref, tri_ref, o_ref, st_ref):
    @pl.when(pl.program_id(1) == 0)
    def _():
        st_ref[...] = jnp.zeros_like(st_ref)

    heads = range(GLA_HEADS)
    chunks = [slice(c * GLA_CHUNK, (c + 1) * GLA_CHUNK) for c in range(GLA_GROUP // GLA_CHUNK)]

    def group(g, carry):
        rows = pl.ds(pl.multiple_of(g * GLA_GROUP, GLA_GROUP), GLA_GROUP)
        ck = [slice(h * GLA_DK, (h + 1) * GLA_DK) for h in heads]
        cv = [slice(h * GLA_DV, (h + 1) * GLA_DV) for h in heads]
        csum = csum_ref[...]
        tri = tri_ref[...] > 0.5

        b = []
        for h in heads:
            la = la_ref[0, rows, ck[h]]
            la_hi = la.astype(BF16)
            la_lo = (la - la_hi.astype(F32)).astype(BF16)
            b.append(_dot(csum, la_hi) + _dot(csum, la_lo))

        def chunk_row(x, row):
            return jnp.concatenate(
                [jnp.broadcast_to(x[cr.start + row:cr.start + row + 1, :], (GLA_CHUNK, x.shape[1]))
                 for cr in chunks], axis=0)

        b_ref = [chunk_row(b[h], GLA_CHUNK // 2 - 1) for h in heads]
        b_last = [chunk_row(b[h], GLA_CHUNK - 1) for h in heads]

        q = [q_ref[0, rows, ck[h]].astype(F32) * (GLA_DK ** -0.5) for h in heads]
        k = [k_ref[0, rows, ck[h]].astype(F32) for h in heads]
        v = [v_ref[0, rows, cv[h]].astype(BF16) for h in heads]
        q_intra = [(q[h] * jnp.exp(b[h] - b_ref[h])).astype(BF16) for h in heads]
        k_intra = [(k[h] * jnp.exp(b_ref[h] - b[h])).astype(BF16) for h in heads]
        q_inter = [(q[h] * jnp.exp(b[h])).astype(BF16) for h in heads]
        k_state = [(k[h] * jnp.exp(b_last[h] - b[h])).astype(BF16) for h in heads]

        scores = [jnp.where(tri, _dot_nt(q_intra[h], k_intra[h]), 0.0).astype(BF16) for h in heads]
        kv = [[_dot_tn(v[h][cr], k_state[h][cr]) for cr in chunks] for h in heads]
        o = [_dot(scores[h], v[h]) for h in heads]

        st = [st_ref[h] for h in heads]
        o_inter = [[] for _ in heads]
        for c, cr in enumerate(chunks):
            for h in heads:
                o_inter[h].append(_dot_nt(q_inter[h][cr], st[h].astype(BF16)))
                decay = jnp.exp(b_last[h][c * GLA_CHUNK:c * GLA_CHUNK + 1, :])
                st[h] = st[h] * decay + kv[h][c]

        gain = gain_ref[...]
        for h in heads:
            st_ref[h] = st[h]
            oh = o[h] + jnp.concatenate(o_inter[h], axis=0)
            oh = oh * _rms_scale(oh) * gain
            r = r_ref[0, rows, cv[h]].astype(F32)
            o_ref[0, rows, cv[h]] = (oh * (r * jax.nn.sigmoid(r))).astype(o_ref.dtype)
        return carry

    lax.fori_loop(0, GLA_LS // GLA_GROUP, group, 0)


def _gla(z3, la3, gain):
    bsz, seq, _ = z3.shape
    csum, tri = _gla_consts()
    const = lambda b, s: (0, 0)
    return pl.pallas_call(
        _gla_body,
        grid=(bsz, seq // GLA_LS),
        in_specs=[
            pl.BlockSpec((1, GLA_LS, GLA_QK), lambda b, s: (b, s, COL_QG // GLA_QK)),
            pl.BlockSpec((1, GLA_LS, GLA_QK), lambda b, s: (b, s, COL_KG // GLA_QK)),
            pl.BlockSpec((1, GLA_LS, GLA_WIDTH), lambda b, s: (b, s, COL_VG // GLA_WIDTH)),
            pl.BlockSpec((1, GLA_LS, GLA_WIDTH), lambda b, s: (b, s, COL_RG // GLA_WIDTH)),
            pl.BlockSpec((1, GLA_LS, GLA_QK), lambda b, s: (b, s, 0)),
            pl.BlockSpec((1, GLA_DV), const),
            pl.BlockSpec((GLA_GROUP, GLA_GROUP), const),
            pl.BlockSpec((GLA_GROUP, GLA_GROUP), const),
        ],
        out_specs=pl.BlockSpec((1, GLA_LS, GLA_WIDTH), lambda b, s: (b, s, 0)),
        out_shape=jax.ShapeDtypeStruct((bsz, seq, GLA_WIDTH), BF16),
        scratch_shapes=[pltpu.VMEM((GLA_HEADS, GLA_DV, GLA_DK), F32)],
        compiler_params=_params(("parallel", "arbitrary")),
        name="gla",
    )(z3, z3, z3, z3, la3, gain, csum, tri)


NEG = -1e30
ATT_MERGE_ROWS = 512
ATT_UNROLL = 32


def _att_consts():
    qi = np.arange(ATT_BLOCK)[:, None]
    kj = np.arange(2 * ATT_BLOCK)[None, :]
    first = kj <= qi
    dist = qi + ATT_BLOCK - kj
    band = (dist >= 0) & (dist <= ATT_BLOCK)
    bias = np.where(np.stack([first, band]), 0.0, NEG).astype(np.float32)
    return jnp.asarray(bias), jnp.ones((2 * ATT_BLOCK, LANES), BF16)


def _att_body(qb_ref, kb_ref, vb_ref, bias_ref, ones_ref, o_ref, qkv_ref, ob_ref, lse_ref, *, seq):
    def widen(i, carry):
        rs = pl.ds(pl.multiple_of(i * ATT_MERGE_ROWS, ATT_MERGE_ROWS), ATT_MERGE_ROWS)
        for a, src in enumerate((qb_ref, kb_ref, vb_ref)):
            qkv_ref[a, rs, :] = src[0, rs, :].astype(F32)
        return carry

    lax.fori_loop(0, seq // ATT_MERGE_ROWS, widen, 0)

    def blocks(br, r, specs):
        def rows(start, size):
            return pl.ds(start, size) if r == 1 else pl.ds(start, size, stride=r)

        ss, vs = [], []
        for q_start, k_start, n_keys, bias in specs:
            q = qkv_ref[0, rows(q_start, ATT_BLOCK), :].astype(BF16)
            kk = qkv_ref[1, rows(k_start, n_keys), :].astype(BF16)
            ss.append(_dot_nt(q, kk) + bias)
            vv = qkv_ref[2, rows(k_start, n_keys), :].astype(BF16)
            vs.append(jnp.concatenate([vv, ones_ref[:n_keys, :]], axis=1))
        ms = [jnp.max(s, axis=-1, keepdims=True) for s in ss]
        ps = [jnp.exp2(s - m).astype(BF16) for s, m in zip(ss, ms)]
        ols = [_dot(p, v) for p, v in zip(ps, vs)]
        for (q_start, _, _, _), m, ol in zip(specs, ms, ols):
            o, l = ol[:, :ATT_HD], ol[:, ATT_HD:]
            ob_ref[br, rows(q_start, ATT_BLOCK), :] = o / l
            lse_ref[br, rows(q_start, ATT_BLOCK), :] = m + jnp.log2(l)

    for br, (window, r) in enumerate(DIL_CONFIGS):
        assert window // r == ATT_BLOCK
        n_blocks = seq // (ATT_BLOCK * r)
        span = ATT_BLOCK * r
        u_c = min(r, ATT_UNROLL)
        u_n = ATT_UNROLL // u_c
        assert n_blocks >= 2 and r % u_c == 0 and n_blocks % u_n == 0
        c_groups = r // u_c

        if u_n == 1:
            def first_row(i, carry, br=br, r=r, u_c=u_c):
                cs = [i * u_c + u for u in range(u_c)]
                blocks(br, r, [(c, c, ATT_BLOCK, bias_ref[0, :, :ATT_BLOCK]) for c in cs])
                return carry

            def later_rows(i, carry, br=br, r=r, span=span, u_c=u_c, c_groups=c_groups):
                n = 1 + i // c_groups
                cs = [(i % c_groups) * u_c + u for u in range(u_c)]
                blocks(br, r, [(c + n * span, c + (n - 1) * span, 2 * ATT_BLOCK, bias_ref[1])
                               for c in cs])
                return carry

            lax.fori_loop(0, c_groups, first_row, 0)
            lax.fori_loop(0, (n_blocks - 1) * c_groups, later_rows, 0)
        else:
            assert c_groups == 1
            def rows_iter(i, carry, br=br, r=r, span=span, u_n=u_n):
                specs = []
                for un in range(u_n):
                    n = i * u_n + un
                    if un == 0:
                        bias = bias_ref[jnp.minimum(n, 1)]
                        k_row = jnp.maximum(n - 1, 0) * span
                    else:
                        bias = bias_ref[1]
                        k_row = (n - 1) * span
                    specs += [(c + n * span, c + k_row, 2 * ATT_BLOCK, bias) for c in range(r)]
                blocks(br, r, specs)
                return carry

            lax.fori_loop(0, n_blocks // u_n, rows_iter, 0)

    def merge(i, carry):
        rs = pl.ds(pl.multiple_of(i * ATT_MERGE_ROWS, ATT_MERGE_ROWS), ATT_MERGE_ROWS)
        lses = [lse_ref[b, rs, :] for b in range(len(DIL_CONFIGS))]
        m = functools.reduce(jnp.maximum, lses)
        ws = [jnp.exp2(x - m) for x in lses]
        den = functools.reduce(lambda a, b: a + b, ws)
        num = functools.reduce(lambda a, b: a + b, [w * ob_ref[b, rs, :] for b, w in enumerate(ws)])
        o_ref[0, rs, :] = (num / den).astype(o_ref.dtype)
        return carry

    lax.fori_loop(0, seq // ATT_MERGE_ROWS, merge, 0)


def _dil_attn(z3):
    bsz, seq, _ = z3.shape
    bias, ones = _att_consts()
    nb = len(DIL_CONFIGS)
    return pl.pallas_call(
        functools.partial(_att_body, seq=seq),
        grid=(bsz, ATT_HEADS),
        in_specs=[
            pl.BlockSpec((1, seq, ATT_HD), lambda b, h: (b, 0, COL_QA // ATT_HD + h)),
            pl.BlockSpec((1, seq, ATT_HD), lambda b, h: (b, 0, COL_KA // ATT_HD + h)),
            pl.BlockSpec((1, seq, ATT_HD), lambda b, h: (b, 0, COL_VA // ATT_HD + h)),
            pl.BlockSpec((2, ATT_BLOCK, 2 * ATT_BLOCK), lambda b, h: (0, 0, 0)),
            pl.BlockSpec((2 * ATT_BLOCK, LANES), lambda b, h: (0, 0)),
        ],
        out_specs=pl.BlockSpec((1, seq, ATT_HD), lambda b, h: (b, 0, h)),
        out_shape=jax.ShapeDtypeStruct((bsz, seq, ATT_WIDTH), BF16),
        scratch_shapes=[pltpu.VMEM((3, seq, ATT_HD), F32),
                        pltpu.VMEM((nb, seq, ATT_HD), F32),
                        pltpu.VMEM((nb, seq, ATT_HD), F32)],
        compiler_params=_params(("parallel", "parallel")),
        name="dil_attn",
    )(z3, z3, z3, bias, ones)


OUT_TM = 512
OUT_TN = D_MODEL


def _out_body(x_ref, og_ref, oa_ref, wg_ref, wa_ref, o_ref):
    o_ref[...] = x_ref[...] + _dot(og_ref[...], wg_ref[...]) + _dot(oa_ref[...], wa_ref[...])


def _out_proj(x2d, og, oa, w_top, w_bot):
    t = x2d.shape[0]
    return pl.pallas_call(
        _out_body,
        grid=(t // OUT_TM, D_MODEL // OUT_TN),
        in_specs=[
            pl.BlockSpec((OUT_TM, OUT_TN), lambda i, j: (i, j)),
            pl.BlockSpec((OUT_TM, GLA_WIDTH), lambda i, j: (i, 0)),
            pl.BlockSpec((OUT_TM, ATT_WIDTH), lambda i, j: (i, 0)),
            pl.BlockSpec((GLA_WIDTH, OUT_TN), lambda i, j: (0, j)),
            pl.BlockSpec((ATT_WIDTH, OUT_TN), lambda i, j: (0, j)),
        ],
        out_specs=pl.BlockSpec((OUT_TM, OUT_TN), lambda i, j: (i, j)),
        out_shape=jax.ShapeDtypeStruct((t, D_MODEL), F32),
        compiler_params=_params(("parallel", "arbitrary")),
        name="out_proj",
    )(x2d, og, oa, w_top, w_bot)


def _rope_tables(seq):
    half = ATT_HD // 2
    inv_freq = 1.0 / (ROPE_THETA ** (np.arange(half, dtype=np.float64) / half))
    ang = np.arange(seq, dtype=np.float64)[:, None] * inv_freq[None, :]
    cos, sin = np.cos(ang), np.sin(ang)
    return (jnp.asarray(np.concatenate([cos, cos], axis=1), F32),
            jnp.asarray(np.concatenate([-sin, sin], axis=1), F32))


def kernel(x, ffn1_norm, ffn1_w_gate, ffn1_w_up, ffn1_w_down, mix_norm, w_in, gla_gate_up, gla_gate_bias, gla_out_norm, att_q_norm, att_k_norm, w_out, ffn2_norm, ffn2_w_gate, ffn2_w_up, ffn2_w_down):
    bsz, seq, d = x.shape
    depth = ffn1_norm.shape[0]
    t = bsz * seq
    cos, sin = _rope_tables(seq)
    x2d = x.reshape(t, d)
    for l in range(depth):
        x2d, w2_gate, w2_up, w2_down, w_main, w_low, w_o = _ffn(
            x2d, ffn1_norm[l][None], ffn1_w_gate[l].astype(BF16), ffn1_w_up[l].astype(BF16),
            ffn1_w_down[l].astype(BF16),
            prepare=(ffn2_w_gate[l], ffn2_w_up[l], ffn2_w_down[l], w_in[l].astype(BF16), w_out[l]))

        gate_up = jnp.pad(gla_gate_up[l], ((0, LANES - GLA_GATE_RANK), (0, 0))).astype(BF16)
        z, la = _in_proj(x2d, mix_norm[l][None], w_main, w_low, gate_up, gla_gate_bias[l][None],
                         att_q_norm[l][None], att_k_norm[l][None], cos, sin, seq)
        z3 = z.reshape(bsz, seq, D_MAIN)
        o_g = _gla(z3, la.reshape(bsz, seq, GLA_QK), gla_out_norm[l][None])
        o_a = _dil_attn(z3)

        x2d = _out_proj(x2d, o_g.reshape(t, GLA_WIDTH), o_a.reshape(t, ATT_WIDTH),
                        w_o[:GLA_WIDTH], w_o[GLA_WIDTH:])

        x2d = _ffn(x2d, ffn2_norm[l][None], w2_gate, w2_up, w2_down)
    return x2d.reshape(bsz, seq, d)
```

```python
import functools

import numpy as np
import jax
import jax.numpy as jnp
from jax import lax
from jax.experimental import pallas as pl
from jax.experimental.pallas import tpu as pltpu

F32 = jnp.float32
BF16 = jnp.bfloat16

D_MODEL = 2048
D_FF = 5632
EPS = 1e-6
FFN_RESIDUAL_WEIGHT = 0.5

GLA_HEADS = 4
GLA_DK = 128
GLA_DV = 256
GLA_QK = GLA_HEADS * GLA_DK
GLA_WIDTH = GLA_HEADS * GLA_DV
GLA_GATE_RANK = 16
GLA_GATE_TAU = 16.0
GLA_CHUNK = 64

ATT_HD = 128
ATT_HEADS = 8
ATT_WIDTH = ATT_HEADS * ATT_HD
DIL_CONFIGS = ((128, 1), (512, 4), (2048, 16))
ATT_BLOCK = 128
ROPE_THETA = 10000.0

VMEM_LIMIT_BYTES = 60 * 1024 * 1024
LANES = 128

COL_QG, COL_KG, COL_VG, COL_RG = 0, 512, 1024, 2048
COL_QA, COL_KA, COL_VA = 3072, 4096, 5120
D_MAIN = 6144
PROJ_TN = 2048


def _params(sem):
    return pltpu.CompilerParams(dimension_semantics=sem, vmem_limit_bytes=VMEM_LIMIT_BYTES)


def _rms_scale(x):
    return lax.rsqrt(jnp.mean(x * x, axis=-1, keepdims=True) + EPS)


def _dot(a, b):
    return jnp.dot(a, b, preferred_element_type=F32)


def _dot_nt(a, b):
    return lax.dot_general(a, b, (((1,), (1,)), ((), ())), preferred_element_type=F32)


def _dot_tn(a, b):
    return lax.dot_general(a, b, (((0,), (0,)), ((), ())), preferred_element_type=F32)


FFN_TM = 1024
FFN_TF = 512


W_IN_ROWS = 16


def _ffn_body(*refs, hosted):
    x_ref, gain_ref, wg_ref, wu_ref, wd_ref = refs[:5]
    if hosted:
        (pg_ref, pu_ref, pd_ref, pwin_ref, pwout_ref,
         o_ref, og_ref, ou_ref, od_ref, omain_ref, olow_ref, owout_ref) = refs[5:17]
    else:
        o_ref = refs[5]
    h_ref, rs_ref = refs[-2:]
    j = pl.program_id(1)

    def prepare_weights():
        for src, dst in ((pg_ref, og_ref), (pu_ref, ou_ref), (pd_ref, od_ref), (pwout_ref, owout_ref)):
            dst[...] = src[...].astype(dst.dtype)
        w = pwin_ref[...]
        omain_ref[:, :COL_RG] = w[:, :COL_RG].astype(BF16)
        omain_ref[:, COL_RG:] = w[:, COL_RG + GLA_GATE_RANK:].astype(BF16)
        low = w[:, COL_RG:COL_RG + LANES]
        lane = lax.broadcasted_iota(jnp.int32, low.shape, 1)
        olow_ref[...] = jnp.where(lane < GLA_GATE_RANK, low, 0.0).astype(BF16)

    def swiglu_down():
        h = h_ref[...]
        rs = jnp.tile(rs_ref[...], (1, FFN_TF // LANES))
        g = _dot(h, wg_ref[...]) * rs
        u = _dot(h, wu_ref[...]) * rs
        a = (g * jax.nn.sigmoid(g) * u * FFN_RESIDUAL_WEIGHT).astype(BF16)
        if hosted:
            prepare_weights()
        return _dot(a, wd_ref[...])

    @pl.when(j == 0)
    def _():
        x = x_ref[...]
        h_ref[...] = (x * gain_ref[...]).astype(BF16)
        rs_ref[...] = jnp.broadcast_to(_rms_scale(x), rs_ref.shape)
        o_ref[...] = x + swiglu_down()

    @pl.when(j > 0)
    def _():
        o_ref[...] += swiglu_down()


def _ffn(x2d, gain, wg, wu, wd, prepare=None):
    t = x2d.shape[0]
    n_i, n_j = t // FFN_TM, D_FF // FFN_TF
    in_specs = [
        pl.BlockSpec((FFN_TM, D_MODEL), lambda i, j: (i, 0)),
        pl.BlockSpec((1, D_MODEL), lambda i, j: (0, 0)),
        pl.BlockSpec((D_MODEL, FFN_TF), lambda i, j: (0, j)),
        pl.BlockSpec((D_MODEL, FFN_TF), lambda i, j: (0, j)),
        pl.BlockSpec((FFN_TF, D_MODEL), lambda i, j: (j, 0)),
    ]
    out_specs = [pl.BlockSpec((FFN_TM, D_MODEL), lambda i, j: (i, 0))]
    out_shape = [jax.ShapeDtypeStruct((t, D_MODEL), F32)]
    args = [x2d, gain, wg, wu, wd]
    if prepare is not None:
        pg, pu, pd, pwin, pwout = prepare
        rows = D_MODEL // n_i
        assert D_MODEL % n_i == 0 and rows % LANES == 0
        assert pg.shape == pu.shape == (D_MODEL, D_FF) and pd.shape == (D_FF, D_MODEL)
        assert pwin.shape == (D_MODEL, D_MAIN + GLA_GATE_RANK) and pwout.shape == (D_MODEL, D_MODEL)
        n_win = D_MODEL // W_IN_ROWS
        assert n_win <= n_i * n_j
        win_idx = lambda i, j: (jnp.minimum(i * n_j + j, n_win - 1), 0)
        up_specs = [pl.BlockSpec((rows, FFN_TF), lambda i, j: (i, j))] * 2
        down_spec = pl.BlockSpec((FFN_TF, rows), lambda i, j: (j, i))
        wout_spec = pl.BlockSpec((W_IN_ROWS, D_MODEL), win_idx)
        in_specs += up_specs + [down_spec, pl.BlockSpec((W_IN_ROWS, pwin.shape[1]), win_idx), wout_spec]
        out_specs += up_specs + [down_spec, pl.BlockSpec((W_IN_ROWS, D_MAIN), win_idx),
                                 pl.BlockSpec((W_IN_ROWS, LANES), win_idx), wout_spec]
        out_shape += [jax.ShapeDtypeStruct(pg.shape, BF16), jax.ShapeDtypeStruct(pu.shape, BF16),
                      jax.ShapeDtypeStruct(pd.shape, BF16),
                      jax.ShapeDtypeStruct((D_MODEL, D_MAIN), BF16),
                      jax.ShapeDtypeStruct((D_MODEL, LANES), BF16),
                      jax.ShapeDtypeStruct(pwout.shape, BF16)]
        args += [pg, pu, pd, pwin, pwout]
    outs = pl.pallas_call(
        functools.partial(_ffn_body, hosted=prepare is not None),
        grid=(n_i, n_j),
        in_specs=in_specs,
        out_specs=out_specs,
        out_shape=out_shape,
        scratch_shapes=[pltpu.VMEM((FFN_TM, D_MODEL), BF16),
                        pltpu.VMEM((FFN_TM, LANES), F32)],
        compiler_params=_params(("arbitrary" if prepare is not None else "parallel", "arbitrary")),
        name="ffn",
    )(*args)
    return outs if prepare is not None else outs[0]


PROJ_TM = 512
PROJ_PIECE = 256
LOG2E = 1.4426950408889634
ATT_Q_SCALE = ATT_HD ** -0.5 * LOG2E


def _log_sigmoid(x):
    return jnp.minimum(x, 0.0) - jnp.log(1.0 + jnp.exp(-jnp.abs(x)))


def _rope_heads(z, gain, cos, sin):
    outs = []
    for hd in range(z.shape[1] // ATT_HD):
        zh = z[:, hd * ATT_HD:(hd + 1) * ATT_HD]
        zh = zh * _rms_scale(zh) * gain
        outs.append(zh * cos + pltpu.roll(zh, ATT_HD // 2, 1) * sin)
    return jnp.concatenate(outs, axis=1)


def _proj_body(x_ref, gain_ref, w_ref, wlow_ref, gup_ref, gbias_ref, qn_ref, kn_ref,
               cos_ref, sin_ref, zg_ref, zr_ref, zq_ref, zkv_ref, la_ref, h_ref, rs_ref):
    step_id = pl.program_id(0)

    def normalise():
        x = x_ref[...]
        h_ref[...] = (x * gain_ref[...]).astype(BF16)
        rs = jnp.broadcast_to(_rms_scale(x), rs_ref.shape)
        rs_ref[...] = rs
        return rs

    def forget_gate(rs):
        g_low = _dot(h_ref[...], wlow_ref[...])
        logit = (_dot(g_low.astype(BF16), gup_ref[...]) * jnp.tile(rs, (1, GLA_QK // LANES))
                 + gbias_ref[...])
        la_ref[...] = _log_sigmoid(logit) * (1.0 / GLA_GATE_TAU)

    def piece(c0, gain, dst, d0):
        cols = slice(c0, c0 + PROJ_PIECE)
        z = _dot(h_ref[...], w_ref[:, cols]) * jnp.tile(rs_ref[...], (1, PROJ_PIECE // LANES))
        if gain is not None:
            z = _rope_heads(z, gain, cos_ref[...], sin_ref[...])
        dst[:, d0:d0 + PROJ_PIECE] = z.astype(dst.dtype)

    outputs = ((COL_QG, zg_ref), (COL_RG, zr_ref), (COL_QA, zq_ref), (COL_KA, zkv_ref))

    def destination(col):
        first, ref = [(f, r) for f, r in outputs if f <= col][-1]
        return ref, col - first

    def has_epilogue(col):
        return COL_QA <= col < COL_VA

    def piece_gain(col):
        if not has_epilogue(col):
            return None
        return qn_ref[...] * ATT_Q_SCALE if col < COL_KA else kn_ref[...]

    for step in range(D_MAIN // PROJ_TN):
        @pl.when(step_id == step)
        def _(step=step):
            rs = normalise()
            if step == 0:
                forget_gate(rs)
            starts = list(range(0, PROJ_TN, PROJ_PIECE))
            starts.sort(key=lambda c0: not has_epilogue(step * PROJ_TN + c0))
            for c0 in starts:
                piece(c0, piece_gain(step * PROJ_TN + c0), *destination(step * PROJ_TN + c0))


def _in_proj(x2d, gain, w_main, w_low, gate_up, gate_bias, q_norm, k_norm, cos, sin, seq):
    t = x2d.shape[0]
    n_tiles = t // PROJ_TM
    n_seq_tiles = seq // PROJ_TM
    const = lambda j, i: (0, 0)
    assert (COL_RG, COL_KA) == (PROJ_TN, 2 * PROJ_TN)

    def written_in(step):
        return lambda j, i: (jnp.where(j == step, i, jnp.where(j < step, 0, n_tiles - 1)), 0)

    return pl.pallas_call(
        _proj_body,
        grid=(D_MAIN // PROJ_TN, n_tiles),
        in_specs=[
            pl.BlockSpec((PROJ_TM, D_MODEL), lambda j, i: (i, 0)),
            pl.BlockSpec((1, D_MODEL), const),
            pl.BlockSpec((D_MODEL, PROJ_TN), lambda j, i: (0, j)),
            pl.BlockSpec((D_MODEL, LANES), const),
            pl.BlockSpec((LANES, GLA_QK), const),
            pl.BlockSpec((1, GLA_QK), const),
            pl.BlockSpec((1, ATT_HD), const),
            pl.BlockSpec((1, ATT_HD), const),
            pl.BlockSpec((PROJ_TM, ATT_HD), lambda j, i: (i % n_seq_tiles, 0)),
            pl.BlockSpec((PROJ_TM, ATT_HD), lambda j, i: (i % n_seq_tiles, 0)),
        ],
        out_specs=[
            pl.BlockSpec((PROJ_TM, COL_RG - COL_QG), written_in(0)),
            pl.BlockSpec((PROJ_TM, COL_QA - COL_RG), written_in(1)),
            pl.BlockSpec((PROJ_TM, COL_KA - COL_QA), written_in(1)),
            pl.BlockSpec((PROJ_TM, D_MAIN - COL_KA), written_in(2)),
            pl.BlockSpec((PROJ_TM, GLA_QK), written_in(0)),
        ],
        out_shape=[
            jax.ShapeDtypeStruct((t, COL_RG - COL_QG), BF16),
            jax.ShapeDtypeStruct((t, COL_QA - COL_RG), BF16),
            jax.ShapeDtypeStruct((t, COL_KA - COL_QA), F32),
            jax.ShapeDtypeStruct((t, D_MAIN - COL_KA), F32),
            jax.ShapeDtypeStruct((t, GLA_QK), F32),
        ],
        scratch_shapes=[pltpu.VMEM((PROJ_TM, D_MODEL), BF16),
                        pltpu.VMEM((PROJ_TM, LANES), F32)],
        compiler_params=_params(("arbitrary", "arbitrary")),
        name="in_proj",
    )(x2d, gain, w_main, w_low, gate_up, gate_bias, q_norm, k_norm, cos, sin)


GLA_LS = 1024
GLA_GROUP = 256


def _gla_consts():
    i = np.arange(GLA_GROUP)
    same = (i[:, None] // GLA_CHUNK) == (i[None, :] // GLA_CHUNK)
    tri = (same & (i[None, :] <= i[:, None])).astype(np.float32)
    return jnp.asarray(tri, BF16), jnp.asarray(tri)


def _gla_body(q_ref, k_ref, v_ref, r_ref, la_ref, gain_ref, csum_ref, tri_ref, o_ref, st_ref):
    @pl.when(pl.program_id(1) == 0)
    def _():
        st_ref[...] = jnp.zeros_like(st_ref)

    heads = range(GLA_HEADS)
    chunks = [slice(c * GLA_CHUNK, (c + 1) * GLA_CHUNK) for c in range(GLA_GROUP // GLA_CHUNK)]

    def group(g, carry):
        rows = pl.ds(pl.multiple_of(g * GLA_GROUP, GLA_GROUP), GLA_GROUP)
        ck = [slice(h * GLA_DK, (h + 1) * GLA_DK) for h in heads]
        cv = [slice(h * GLA_DV, (h + 1) * GLA_DV) for h in heads]
        csum = csum_ref[...]
        tri = tri_ref[...] > 0.5

        b = []
        for h in heads:
            la = la_ref[0, rows, ck[h]]
            la_hi = la.astype(BF16)
            la_lo = (la - la_hi.astype(F32)).astype(BF16)
            b.append(_dot(csum, la_hi) + _dot(csum, la_lo))

        def chunk_row(x, row):
            return jnp.concatenate(
                [jnp.broadcast_to(x[cr.start + row:cr.start + row + 1, :], (GLA_CHUNK, x.shape[1]))
                 for cr in chunks], axis=0)

        b_ref = [chunk_row(b[h], GLA_CHUNK // 2 - 1) for h in heads]
        b_last = [chunk_row(b[h], GLA_CHUNK - 1) for h in heads]

        q = [q_ref[0, rows, ck[h]].astype(F32) * (GLA_DK ** -0.5) for h in heads]
        k = [k_ref[0, rows, ck[h]].astype(F32) for h in heads]
        v = [v_ref[0, rows, cv[h]].astype(BF16) for h in heads]
        q_intra = [(q[h] * jnp.exp(b[h] - b_ref[h])).astype(BF16) for h in heads]
        k_intra = [(k[h] * jnp.exp(b_ref[h] - b[h])).astype(BF16) for h in heads]
        q_inter = [(q[h] * jnp.exp(b[h])).astype(BF16) for h in heads]
        k_state = [(k[h] * jnp.exp(b_last[h] - b[h])).astype(BF16) for h in heads]

        scores = [jnp.where(tri, _dot_nt(q_intra[h], k_intra[h]), 0.0).astype(BF16) for h in heads]
        kv = [[_dot_tn(v[h][cr], k_state[h][cr]) for cr in chunks] for h in heads]
        o = [_dot(scores[h], v[h]) for h in heads]

        st = [st_ref[h] for h in heads]
        o_inter = [[] for _ in heads]
        for c, cr in enumerate(chunks):
            for h in heads:
                o_inter[h].append(_dot_nt(q_inter[h][cr], st[h].astype(BF16)))
                decay = jnp.exp(b_last[h][c * GLA_CHUNK:c * GLA_CHUNK + 1, :])
                st[h] = st[h] * decay + kv[h][c]

        gain = gain_ref[...]
        for h in heads:
            st_ref[h] = st[h]
            oh = o[h] + jnp.concatenate(o_inter[h], axis=0)
            oh = oh * _rms_scale(oh) * gain
            r = r_ref[0, rows, cv[h]].astype(F32)
            o_ref[0, rows, cv[h]] = (oh * (r * jax.nn.sigmoid(r))).astype(o_ref.dtype)
        return carry

    lax.fori_loop(0, GLA_LS // GLA_GROUP, group, 0)


def _gla(zg, zr, la3, gain):
    bsz, seq, _ = zg.shape
    csum, tri = _gla_consts()
    const = lambda b, s: (0, 0)
    return pl.pallas_call(
        _gla_body,
        grid=(bsz, seq // GLA_LS),
        in_specs=[
            pl.BlockSpec((1, GLA_LS, GLA_QK), lambda b, s: (b, s, COL_QG // GLA_QK)),
            pl.BlockSpec((1, GLA_LS, GLA_QK), lambda b, s: (b, s, COL_KG // GLA_QK)),
            pl.BlockSpec((1, GLA_LS, GLA_WIDTH), lambda b, s: (b, s, COL_VG // GLA_WIDTH)),
            pl.BlockSpec((1, GLA_LS, GLA_WIDTH), lambda b, s: (b, s, 0)),
            pl.BlockSpec((1, GLA_LS, GLA_QK), lambda b, s: (b, s, 0)),
            pl.BlockSpec((1, GLA_DV), const),
            pl.BlockSpec((GLA_GROUP, GLA_GROUP), const),
            pl.BlockSpec((GLA_GROUP, GLA_GROUP), const),
        ],
        out_specs=pl.BlockSpec((1, GLA_LS, GLA_WIDTH), lambda b, s: (b, s, 0)),
        out_shape=jax.ShapeDtypeStruct((bsz, seq, GLA_WIDTH), BF16),
        scratch_shapes=[pltpu.VMEM((GLA_HEADS, GLA_DV, GLA_DK), F32)],
        compiler_params=_params(("parallel", "arbitrary")),
        name="gla",
    )(zg, zg, zg, zr, la3, gain, csum, tri)


NEG = -1e30
ATT_MERGE_ROWS = 512
ATT_UNROLL = 32


def _att_consts():
    qi = np.arange(ATT_BLOCK)[:, None]
    kj = np.arange(2 * ATT_BLOCK)[None, :]
    first = kj <= qi
    dist = qi + ATT_BLOCK - kj
    band = (dist >= 0) & (dist <= ATT_BLOCK)
    bias = np.where(np.stack([first, band]), 0.0, NEG).astype(np.float32)
    return jnp.asarray(bias), jnp.ones((2 * ATT_BLOCK, LANES), BF16)


def _att_body(q_ref, k_ref, v_ref, bias_ref, ones_ref, o_ref, ob_ref, lse_ref, *, seq):
    def blocks(br, r, specs):
        def rows(start, size):
            return pl.ds(start, size) if r == 1 else pl.ds(start, size, stride=r)

        ss, vs = [], []
        for q_start, k_start, n_keys, bias in specs:
            q = q_ref[0, rows(q_start, ATT_BLOCK), :].astype(BF16)
            kk = k_ref[0, rows(k_start, n_keys), :].astype(BF16)
            ss.append(_dot_nt(q, kk) + bias)
            vv = v_ref[0, rows(k_start, n_keys), :].astype(BF16)
            vs.append(jnp.concatenate([vv, ones_ref[:n_keys, :]], axis=1))
        ms = [jnp.max(s, axis=-1, keepdims=True) for s in ss]
        ps = [jnp.exp2(s - m).astype(BF16) for s, m in zip(ss, ms)]
        ols = [_dot(p, v) for p, v in zip(ps, vs)]
        for (q_start, _, _, _), m, ol in zip(specs, ms, ols):
            o, l = ol[:, :ATT_HD], ol[:, ATT_HD:]
            ob_ref[br, rows(q_start, ATT_BLOCK), :] = o / l
            lse_ref[br, rows(q_start, ATT_BLOCK), :] = m + jnp.log2(l)

    for br, (window, r) in enumerate(DIL_CONFIGS):
        assert window // r == ATT_BLOCK
        n_blocks = seq // (ATT_BLOCK * r)
        span = ATT_BLOCK * r
        u_c = min(r, ATT_UNROLL)
        u_n = ATT_UNROLL // u_c
        assert n_blocks >= 2 and r % u_c == 0 and n_blocks % u_n == 0
        c_groups = r // u_c

        if u_n == 1:
            def first_row(i, carry, br=br, r=r, u_c=u_c):
                cs = [i * u_c + u for u in range(u_c)]
                blocks(br, r, [(c, c, ATT_BLOCK, bias_ref[0, :, :ATT_BLOCK]) for c in cs])
                return carry

            def later_rows(i, carry, br=br, r=r, span=span, u_c=u_c, c_groups=c_groups):
                n = 1 + i // c_groups
                cs = [(i % c_groups) * u_c + u for u in range(u_c)]
                blocks(br, r, [(c + n * span, c + (n - 1) * span, 2 * ATT_BLOCK, bias_ref[1])
                               for c in cs])
                return carry

            lax.fori_loop(0, c_groups, first_row, 0)
            lax.fori_loop(0, (n_blocks - 1) * c_groups, later_rows, 0)
        else:
            assert c_groups == 1
            def rows_iter(i, carry, br=br, r=r, span=span, u_n=u_n):
                specs = []
                for un in range(u_n):
                    n = i * u_n + un
                    if un == 0:
                        bias = bias_ref[jnp.minimum(n, 1)]
                        k_row = jnp.maximum(n - 1, 0) * span
                    else:
                        bias = bias_ref[1]
                        k_row = (n - 1) * span
                    specs += [(c + n * span, c + k_row, 2 * ATT_BLOCK, bias) for c in range(r)]
                blocks(br, r, specs)
                return carry

            lax.fori_loop(0, n_blocks // u_n, rows_iter, 0)

    def merge(i, carry):
        rs = pl.ds(pl.multiple_of(i * ATT_MERGE_ROWS, ATT_MERGE_ROWS), ATT_MERGE_ROWS)
        lses = [lse_ref[b, rs, :] for b in range(len(DIL_CONFIGS))]
        m = functools.reduce(jnp.maximum, lses)
        ws = [jnp.exp2(x - m) for x in lses]
        den = functools.reduce(lambda a, b: a + b, ws)
        num = functools.reduce(lambda a, b: a + b, [w * ob_ref[b, rs, :] for b, w in enumerate(ws)])
        o_ref[0, rs, :] = (num / den).astype(o_ref.dtype)
        return carry

    lax.fori_loop(0, seq // ATT_MERGE_ROWS, merge, 0)


def _dil_attn(zq, zkv):
    bsz, seq, _ = zq.shape
    bias, ones = _att_consts()
    nb = len(DIL_CONFIGS)
    return pl.pallas_call(
        functools.partial(_att_body, seq=seq),
        grid=(bsz, ATT_HEADS),
        in_specs=[
            pl.BlockSpec((1, seq, ATT_HD), lambda b, h: (b, 0, h)),
            pl.BlockSpec((1, seq, ATT_HD), lambda b, h: (b, 0, h)),
            pl.BlockSpec((1, seq, ATT_HD), lambda b, h: (b, 0, (COL_VA - COL_KA) // ATT_HD + h)),
            pl.BlockSpec((2, ATT_BLOCK, 2 * ATT_BLOCK), lambda b, h: (0, 0, 0)),
            pl.BlockSpec((2 * ATT_BLOCK, LANES), lambda b, h: (0, 0)),
        ],
        out_specs=pl.BlockSpec((1, seq, ATT_HD), lambda b, h: (b, 0, h)),
        out_shape=jax.ShapeDtypeStruct((bsz, seq, ATT_WIDTH), BF16),
        scratch_shapes=[pltpu.VMEM((nb, seq, ATT_HD), F32),
                        pltpu.VMEM((nb, seq, ATT_HD), F32)],
        compiler_params=_params(("parallel", "parallel")),
        name="dil_attn",
    )(zq, zkv, zkv, bias, ones)


OUT_TM = 512
OUT_TN = D_MODEL


def _out_body(x_ref, og_ref, oa_ref, wg_ref, wa_ref, o_ref):
    o_ref[...] = x_ref[...] + _dot(og_ref[...], wg_ref[...]) + _dot(oa_ref[...], wa_ref[...])


def _out_proj(x2d, og, oa, w_top, w_bot):
    t = x2d.shape[0]
    return pl.pallas_call(
        _out_body,
        grid=(t // OUT_TM, D_MODEL // OUT_TN),
        in_specs=[
            pl.BlockSpec((OUT_TM, OUT_TN), lambda i, j: (i, j)),
            pl.BlockSpec((OUT_TM, GLA_WIDTH), lambda i, j: (i, 0)),
            pl.BlockSpec((OUT_TM, ATT_WIDTH), lambda i, j: (i, 0)),
            pl.BlockSpec((GLA_WIDTH, OUT_TN), lambda i, j: (0, j)),
            pl.BlockSpec((ATT_WIDTH, OUT_TN), lambda i, j: (0, j)),
        ],
        out_specs=pl.BlockSpec((OUT_TM, OUT_TN), lambda i, j: (i, j)),
        out_shape=jax.ShapeDtypeStruct((t, D_MODEL), F32),
        compiler_params=_params(("parallel", "arbitrary")),
        name="out_proj",
    )(x2d, og, oa, w_top, w_bot)


def _rope_tables(seq):
    half = ATT_HD // 2
    inv_freq = 1.0 / (ROPE_THETA ** (np.arange(half, dtype=np.float64) / half))
    ang = np.arange(seq, dtype=np.float64)[:, None] * inv_freq[None, :]
    cos, sin = np.cos(ang), np.sin(ang)
    return (jnp.asarray(np.concatenate([cos, cos], axis=1), F32),
            jnp.asarray(np.concatenate([-sin, sin], axis=1), F32))


def kernel(x, ffn1_norm, ffn1_w_gate, ffn1_w_up, ffn1_w_down, mix_norm, w_in, gla_gate_up, gla_gate_bias, gla_out_norm, att_q_norm, att_k_norm, w_out, ffn2_norm, ffn2_w_gate, ffn2_w_up, ffn2_w_down):
    bsz, seq, d = x.shape
    depth = ffn1_norm.shape[0]
    t = bsz * seq
    cos, sin = _rope_tables(seq)
    x2d = x.reshape(t, d)
    for l in range(depth):
        x2d, w2_gate, w2_up, w2_down, w_main, w_low, w_o = _ffn(
            x2d, ffn1_norm[l][None], ffn1_w_gate[l].astype(BF16), ffn1_w_up[l].astype(BF16),
            ffn1_w_down[l].astype(BF16),
            prepare=(ffn2_w_gate[l], ffn2_w_up[l], ffn2_w_down[l], w_in[l].astype(BF16), w_out[l]))

        gate_up = jnp.pad(gla_gate_up[l], ((0, LANES - GLA_GATE_RANK), (0, 0))).astype(BF16)
        zg, zr, zq, zkv, la = [
            a.reshape(bsz, seq, -1) for a in _in_proj(
                x2d, mix_norm[l][None], w_main, w_low, gate_up, gla_gate_bias[l][None],
                att_q_norm[l][None], att_k_norm[l][None], cos, sin, seq)]
        o_g = _gla(zg, zr, la, gla_out_norm[l][None])
        o_a = _dil_attn(zq, zkv)

        x2d = _out_proj(x2d, o_g.reshape(t, GLA_WIDTH), o_a.reshape(t, ATT_WIDTH),
                        w_o[:GLA_WIDTH], w_o[GLA_WIDTH:])

        x2d = _ffn(x2d, ffn2_norm[l][None], w2_gate, w2_up, w2_down)
    return x2d.reshape(bsz, seq, d)
```

```python
import functools

import numpy as np
import jax
import jax.numpy as jnp
from jax import lax
from jax.experimental import pallas as pl
from jax.experimental.pallas import tpu as pltpu

F32 = jnp.float32
BF16 = jnp.bfloat16

D_MODEL = 2048
D_FF = 5632
EPS = 1e-6
FFN_RESIDUAL_WEIGHT = 0.5

GLA_HEADS = 4
GLA_DK = 128
GLA_DV = 256
GLA_QK = GLA_HEADS * GLA_DK
GLA_WIDTH = GLA_HEADS * GLA_DV
GLA_GATE_RANK = 16
GLA_GATE_TAU = 16.0
GLA_CHUNK = 64

ATT_HD = 128
ATT_HEADS = 8
ATT_WIDTH = ATT_HEADS * ATT_HD
DIL_CONFIGS = ((128, 1), (512, 4), (2048, 16))
ATT_BLOCK = 128
ROPE_THETA = 10000.0

VMEM_LIMIT_BYTES = 60 * 1024 * 1024
LANES = 128

COL_QG, COL_KG, COL_VG, COL_RG = 0, 512, 1024, 2048
COL_QA, COL_KA, COL_VA = 3072, 4096, 5120
D_MAIN = 6144
PROJ_TN = 2048


def _params(sem):
    return pltpu.CompilerParams(dimension_semantics=sem, vmem_limit_bytes=VMEM_LIMIT_BYTES)


def _rms_scale(x):
    return lax.rsqrt(jnp.mean(x * x, axis=-1, keepdims=True) + EPS)


def _dot(a, b):
    return jnp.dot(a, b, preferred_element_type=F32)


def _dot_nt(a, b):
    return lax.dot_general(a, b, (((1,), (1,)), ((), ())), preferred_element_type=F32)


def _dot_tn(a, b):
    return lax.dot_general(a, b, (((0,), (0,)), ((), ())), preferred_element_type=F32)


FFN_TM = 1024
FFN_TF = 512


W_IN_ROWS = 16


def _ffn_body(*refs, hosted):
    x_ref, gain_ref, wg_ref, wu_ref, wd_ref = refs[:5]
    if hosted:
        (pg_ref, pu_ref, pd_ref, pwin_ref, pwout_ref,
         o_ref, og_ref, ou_ref, od_ref, omain_ref, olow_ref, owout_ref) = refs[5:17]
    else:
        o_ref = refs[5]
    h_ref, rs_ref = refs[-2:]
    j = pl.program_id(1)

    def prepare_weights():
        for src, dst in ((pg_ref, og_ref), (pu_ref, ou_ref), (pd_ref, od_ref), (pwout_ref, owout_ref)):
            dst[...] = src[...].astype(dst.dtype)
        w = pwin_ref[...]
        omain_ref[:, :COL_RG] = w[:, :COL_RG].astype(BF16)
        omain_ref[:, COL_RG:] = w[:, COL_RG + GLA_GATE_RANK:].astype(BF16)
        low = w[:, COL_RG:COL_RG + LANES]
        lane = lax.broadcasted_iota(jnp.int32, low.shape, 1)
        olow_ref[...] = jnp.where(lane < GLA_GATE_RANK, low, 0.0).astype(BF16)

    def swiglu_down():
        h = h_ref[...]
        rs = jnp.tile(rs_ref[...], (1, FFN_TF // LANES))
        g = _dot(h, wg_ref[...]) * rs
        u = _dot(h, wu_ref[...]) * rs
        a = (g * jax.nn.sigmoid(g) * u * FFN_RESIDUAL_WEIGHT).astype(BF16)
        if hosted:
            prepare_weights()
        return _dot(a, wd_ref[...])

    @pl.when(j == 0)
    def _():
        x = x_ref[...]
        h_ref[...] = (x * gain_ref[...]).astype(BF16)
        rs_ref[...] = jnp.broadcast_to(_rms_scale(x), rs_ref.shape)
        o_ref[...] = x + swiglu_down()

    @pl.when(j > 0)
    def _():
        o_ref[...] += swiglu_down()


def _ffn(x2d, gain, wg, wu, wd, prepare=None):
    t = x2d.shape[0]
    n_i, n_j = t // FFN_TM, D_FF // FFN_TF
    in_specs = [
        pl.BlockSpec((FFN_TM, D_MODEL), lambda i, j: (i, 0)),
        pl.BlockSpec((1, D_MODEL), lambda i, j: (0, 0)),
        pl.BlockSpec((D_MODEL, FFN_TF), lambda i, j: (0, j)),
        pl.BlockSpec((D_MODEL, FFN_TF), lambda i, j: (0, j)),
        pl.BlockSpec((FFN_TF, D_MODEL), lambda i, j: (j, 0)),
    ]
    out_specs = [pl.BlockSpec((FFN_TM, D_MODEL), lambda i, j: (i, 0))]
    out_shape = [jax.ShapeDtypeStruct((t, D_MODEL), F32)]
    args = [x2d, gain, wg, wu, wd]
    if prepare is not None:
        pg, pu, pd, pwin, pwout = prepare
        rows = D_MODEL // n_i
        assert D_MODEL % n_i == 0 and rows % LANES == 0
        assert pg.shape == pu.shape == (D_MODEL, D_FF) and pd.shape == (D_FF, D_MODEL)
        assert pwin.shape == (D_MODEL, D_MAIN + GLA_GATE_RANK) and pwout.shape == (D_MODEL, D_MODEL)
        n_win = D_MODEL // W_IN_ROWS
        assert n_win <= n_i * n_j
        win_idx = lambda i, j: (jnp.minimum(i * n_j + j, n_win - 1), 0)
        up_specs = [pl.BlockSpec((rows, FFN_TF), lambda i, j: (i, j))] * 2
        down_spec = pl.BlockSpec((FFN_TF, rows), lambda i, j: (j, i))
        wout_spec = pl.BlockSpec((W_IN_ROWS, D_MODEL), win_idx)
        in_specs += up_specs + [down_spec, pl.BlockSpec((W_IN_ROWS, pwin.shape[1]), win_idx), wout_spec]
        out_specs += up_specs + [down_spec, pl.BlockSpec((W_IN_ROWS, D_MAIN), win_idx),
                                 pl.BlockSpec((W_IN_ROWS, LANES), win_idx), wout_spec]
        out_shape += [jax.ShapeDtypeStruct(pg.shape, BF16), jax.ShapeDtypeStruct(pu.shape, BF16),
                      jax.ShapeDtypeStruct(pd.shape, BF16),
                      jax.ShapeDtypeStruct((D_MODEL, D_MAIN), BF16),
                      jax.ShapeDtypeStruct((D_MODEL, LANES), BF16),
                      jax.ShapeDtypeStruct(pwout.shape, BF16)]
        args += [pg, pu, pd, pwin, pwout]
    outs = pl.pallas_call(
        functools.partial(_ffn_body, hosted=prepare is not None),
        grid=(n_i, n_j),
        in_specs=in_specs,
        out_specs=out_specs,
        out_shape=out_shape,
        scratch_shapes=[pltpu.VMEM((FFN_TM, D_MODEL), BF16),
                        pltpu.VMEM((FFN_TM, LANES), F32)],
        compiler_params=_params(("arbitrary" if prepare is not None else "parallel", "arbitrary")),
        name="ffn",
    )(*args)
    return outs if prepare is not None else outs[0]


PROJ_TM = 512
PROJ_PIECE = 256
LOG2E = 1.4426950408889634
ATT_Q_SCALE = ATT_HD ** -0.5 * LOG2E


def _log_sigmoid(x):
    return jnp.minimum(x, 0.0) - jnp.log(1.0 + jnp.exp(-jnp.abs(x)))


def _rope_heads(z, gain, cos, sin):
    outs = []
    for hd in range(z.shape[1] // ATT_HD):
        zh = z[:, hd * ATT_HD:(hd + 1) * ATT_HD]
        zh = zh * _rms_scale(zh) * gain
        outs.append(zh * cos + pltpu.roll(zh, ATT_HD // 2, 1) * sin)
    return jnp.concatenate(outs, axis=1)


def _proj_body(x_ref, gain_ref, w_ref, wlow_ref, gup_ref, gbias_ref, qn_ref, kn_ref,
               cos_ref, sin_ref, z_ref, la_ref, h_ref, rs_ref):
    step_id = pl.program_id(0)

    def normalise():
        x = x_ref[...]
        h_ref[...] = (x * gain_ref[...]).astype(BF16)
        rs = jnp.broadcast_to(_rms_scale(x), rs_ref.shape)
        rs_ref[...] = rs
        return rs

    def forget_gate(rs):
        g_low = _dot(h_ref[...], wlow_ref[...])
        logit = (_dot(g_low.astype(BF16), gup_ref[...]) * jnp.tile(rs, (1, GLA_QK // LANES))
                 + gbias_ref[...])
        la_ref[...] = _log_sigmoid(logit) * (1.0 / GLA_GATE_TAU)

    def piece(c0, gain):
        cols = slice(c0, c0 + PROJ_PIECE)
        z = _dot(h_ref[...], w_ref[:, cols]) * jnp.tile(rs_ref[...], (1, PROJ_PIECE // LANES))
        if gain is not None:
            z = _rope_heads(z, gain, cos_ref[...], sin_ref[...])
        z_ref[:, cols] = z.astype(z_ref.dtype)

    def has_epilogue(col):
        return COL_QA <= col < COL_VA

    def piece_gain(col):
        if not has_epilogue(col):
            return None
        return qn_ref[...] * ATT_Q_SCALE if col < COL_KA else kn_ref[...]

    for step in range(D_MAIN // PROJ_TN):
        @pl.when(step_id == step)
        def _(step=step):
            rs = normalise()
            if step == 0:
                forget_gate(rs)
            starts = list(range(0, PROJ_TN, PROJ_PIECE))
            starts.sort(key=lambda c0: not has_epilogue(step * PROJ_TN + c0))
            for c0 in starts:
                piece(c0, piece_gain(step * PROJ_TN + c0))


def _in_proj(x2d, gain, w_main, w_low, gate_up, gate_bias, q_norm, k_norm, cos, sin, seq):
    t = x2d.shape[0]
    n_tiles = t // PROJ_TM
    n_seq_tiles = seq // PROJ_TM
    const = lambda j, i: (0, 0)
    return pl.pallas_call(
        _proj_body,
        grid=(D_MAIN // PROJ_TN, n_tiles),
        in_specs=[
            pl.BlockSpec((PROJ_TM, D_MODEL), lambda j, i: (i, 0)),
            pl.BlockSpec((1, D_MODEL), const),
            pl.BlockSpec((D_MODEL, PROJ_TN), lambda j, i: (0, j)),
            pl.BlockSpec((D_MODEL, LANES), const),
            pl.BlockSpec((LANES, GLA_QK), const),
            pl.BlockSpec((1, GLA_QK), const),
            pl.BlockSpec((1, ATT_HD), const),
            pl.BlockSpec((1, ATT_HD), const),
            pl.BlockSpec((PROJ_TM, ATT_HD), lambda j, i: (i % n_seq_tiles, 0)),
            pl.BlockSpec((PROJ_TM, ATT_HD), lambda j, i: (i % n_seq_tiles, 0)),
        ],
        out_specs=[
            pl.BlockSpec((PROJ_TM, PROJ_TN), lambda j, i: (i, j)),
            pl.BlockSpec((PROJ_TM, GLA_QK), lambda j, i: (jnp.where(j == 0, i, n_tiles - 1), 0)),
        ],
        out_shape=[
            jax.ShapeDtypeStruct((t, D_MAIN), BF16),
            jax.ShapeDtypeStruct((t, GLA_QK), F32),
        ],
        scratch_shapes=[pltpu.VMEM((PROJ_TM, D_MODEL), BF16),
                        pltpu.VMEM((PROJ_TM, LANES), F32)],
        compiler_params=_params(("arbitrary", "arbitrary")),
        name="in_proj",
    )(x2d, gain, w_main, w_low, gate_up, gate_bias, q_norm, k_norm, cos, sin)


GLA_LS = 1024
GLA_GROUP = 256


def _gla_consts():
    i = np.arange(GLA_GROUP)
    same = (i[:, None] // GLA_CHUNK) == (i[None, :] // GLA_CHUNK)
    tri = (same & (i[None, :] <= i[:, None])).astype(np.float32)
    return jnp.asarray(tri, BF16), jnp.asarray(tri)


def _gla_body(q_ref, k_ref, v_ref, r_ref, la_ref, gain_ref, csum_ref, tri_ref, o_ref, st_ref):
    @pl.when(pl.program_id(1) == 0)
    def _():
        st_ref[...] = jnp.zeros_like(st_ref)

    heads = range(GLA_HEADS)
    chunks = [slice(c * GLA_CHUNK, (c + 1) * GLA_CHUNK) for c in range(GLA_GROUP // GLA_CHUNK)]

    def group(g, carry):
        rows = pl.ds(pl.multiple_of(g * GLA_GROUP, GLA_GROUP), GLA_GROUP)
        ck = [slice(h * GLA_DK, (h + 1) * GLA_DK) for h in heads]
        cv = [slice(h * GLA_DV, (h + 1) * GLA_DV) for h in heads]
        csum = csum_ref[...]
        tri = tri_ref[...] > 0.5

        b = []
        for h in heads:
            la = la_ref[0, rows, ck[h]]
            la_hi = la.astype(BF16)
            la_lo = (la - la_hi.astype(F32)).astype(BF16)
            b.append(_dot(csum, la_hi) + _dot(csum, la_lo))

        def chunk_row(x, row):
            return jnp.concatenate(
                [jnp.broadcast_to(x[cr.start + row:cr.start + row + 1, :], (GLA_CHUNK, x.shape[1]))
                 for cr in chunks], axis=0)

        b_ref = [chunk_row(b[h], GLA_CHUNK // 2 - 1) for h in heads]
        b_last = [chunk_row(b[h], GLA_CHUNK - 1) for h in heads]

        q = [q_ref[0, rows, ck[h]].astype(F32) * (GLA_DK ** -0.5) for h in heads]
        k = [k_ref[0, rows, ck[h]].astype(F32) for h in heads]
        v = [v_ref[0, rows, cv[h]].astype(BF16) for h in heads]
        q_intra = [(q[h] * jnp.exp(b[h] - b_ref[h])).astype(BF16) for h in heads]
        k_intra = [(k[h] * jnp.exp(b_ref[h] - b[h])).astype(BF16) for h in heads]
        q_inter = [(q[h] * jnp.exp(b[h])).astype(BF16) for h in heads]
        k_state = [(k[h] * jnp.exp(b_last[h] - b[h])).astype(BF16) for h in heads]

        scores = [jnp.where(tri, _dot_nt(q_intra[h], k_intra[h]), 0.0).astype(BF16) for h in heads]
        kv = [[_dot_tn(v[h][cr], k_state[h][cr]) for cr in chunks] for h in heads]
        o = [_dot(scores[h], v[h]) for h in heads]

        st = [st_ref[h] for h in heads]
        o_inter = [[] for _ in heads]
        for c, cr in enumerate(chunks):
            for h in heads:
                o_inter[h].append(_dot_nt(q_inter[h][cr], st[h].astype(BF16)))
                decay = jnp.exp(b_last[h][c * GLA_CHUNK:c * GLA_CHUNK + 1, :])
                st[h] = st[h] * decay + kv[h][c]

        gain = gain_ref[...]
        for h in heads:
            st_ref[h] = st[h]
            oh = o[h] + jnp.concatenate(o_inter[h], axis=0)
            oh = oh * _rms_scale(oh) * gain
            r = r_ref[0, rows, cv[h]].astype(F32)
            o_ref[0, rows, cv[h]] = (oh * (r * jax.nn.sigmoid(r))).astype(o_ref.dtype)
        return carry

    lax.fori_loop(0, GLA_LS // GLA_GROUP, group, 0)


def _gla(z3, la3, gain):
    bsz, seq, _ = z3.shape
    csum, tri = _gla_consts()
    const = lambda b, s: (0, 0)
    return pl.pallas_call(
        _gla_body,
        grid=(bsz, seq // GLA_LS),
        in_specs=[
            pl.BlockSpec((1, GLA_LS, GLA_QK), lambda b, s: (b, s, COL_QG // GLA_QK)),
            pl.BlockSpec((1, GLA_LS, GLA_QK), lambda b, s: (b, s, COL_KG // GLA_QK)),
            pl.BlockSpec((1, GLA_LS, GLA_WIDTH), lambda b, s: (b, s, COL_VG // GLA_WIDTH)),
            pl.BlockSpec((1, GLA_LS, GLA_WIDTH), lambda b, s: (b, s, COL_RG // GLA_WIDTH)),
            pl.BlockSpec((1, GLA_LS, GLA_QK), lambda b, s: (b, s, 0)),
            pl.BlockSpec((1, GLA_DV), const),
            pl.BlockSpec((GLA_GROUP, GLA_GROUP), const),
            pl.BlockSpec((GLA_GROUP, GLA_GROUP), const),
        ],
        out_specs=pl.BlockSpec((1, GLA_LS, GLA_WIDTH), lambda b, s: (b, s, 0)),
        out_shape=jax.ShapeDtypeStruct((bsz, seq, GLA_WIDTH), BF16),
        scratch_shapes=[pltpu.VMEM((GLA_HEADS, GLA_DV, GLA_DK), F32)],
        compiler_params=_params(("parallel", "arbitrary")),
        name="gla",
    )(z3, z3, z3, z3, la3, gain, csum, tri)


NEG = -1e30
ATT_MERGE_ROWS = 512
ATT_UNROLL = 32


def _att_consts():
    qi = np.arange(ATT_BLOCK)[:, None]
    kj = np.arange(2 * ATT_BLOCK)[None, :]
    first = kj <= qi
    dist = qi + ATT_BLOCK - kj
    band = (dist >= 0) & (dist <= ATT_BLOCK)
    bias = np.where(np.stack([first, band]), 0.0, NEG).astype(np.float32)
    return jnp.asarray(bias), jnp.ones((2 * ATT_BLOCK, LANES), BF16)


def _att_body(qb_ref, kb_ref, vb_ref, bias_ref, ones_ref, o_ref, qkv_ref, ob_ref, lse_ref, *, seq):
    def widen(i, carry):
        rs = pl.ds(pl.multiple_of(i * ATT_MERGE_ROWS, ATT_MERGE_ROWS), ATT_MERGE_ROWS)
        for a, src in enumerate((qb_ref, kb_ref, vb_ref)):
            qkv_ref[a, rs, :] = src[0, rs, :].astype(F32)
        return carry

    lax.fori_loop(0, seq // ATT_MERGE_ROWS, widen, 0)

    def blocks(br, r, specs):
        def rows(start, size):
            return pl.ds(start, size) if r == 1 else pl.ds(start, size, stride=r)

        ss, vs = [], []
        for q_start, k_start, n_keys, bias in specs:
            q = qkv_ref[0, rows(q_start, ATT_BLOCK), :].astype(BF16)
            kk = qkv_ref[1, rows(k_start, n_keys), :].astype(BF16)
            ss.append(_dot_nt(q, kk) + bias)
            vv = qkv_ref[2, rows(k_start, n_keys), :].astype(BF16)
            vs.append(jnp.concatenate([vv, ones_ref[:n_keys, :]], axis=1))
        ms = [jnp.max(s, axis=-1, keepdims=True) for s in ss]
        ps = [jnp.exp2(s - m).astype(BF16) for s, m in zip(ss, ms)]
        ols = [_dot(p, v) for p, v in zip(ps, vs)]
        for (q_start, _, _, _), m, ol in zip(specs, ms, ols):
            o, l = ol[:, :ATT_HD], ol[:, ATT_HD:]
            ob_ref[br, rows(q_start, ATT_BLOCK), :] = o / l
            lse_ref[br, rows(q_start, ATT_BLOCK), :] = m + jnp.log2(l)

    for br, (window, r) in reversed(list(enumerate(DIL_CONFIGS))):
        assert window // r == ATT_BLOCK
        n_blocks = seq // (ATT_BLOCK * r)
        span = ATT_BLOCK * r
        u_c = min(r, ATT_UNROLL)
        u_n = ATT_UNROLL // u_c
        assert n_blocks >= 2 and r % u_c == 0 and n_blocks % u_n == 0
        c_groups = r // u_c

        if u_n == 1:
            def first_row(i, carry, br=br, r=r, u_c=u_c):
                cs = [i * u_c + u for u in range(u_c)]
                blocks(br, r, [(c, c, ATT_BLOCK, bias_ref[0, :, :ATT_BLOCK]) for c in cs])
                return carry

            def later_rows(i, carry, br=br, r=r, span=span, u_c=u_c, c_groups=c_groups):
                n = 1 + i // c_groups
                cs = [(i % c_groups) * u_c + u for u in range(u_c)]
                blocks(br, r, [(c + n * span, c + (n - 1) * span, 2 * ATT_BLOCK, bias_ref[1])
                               for c in cs])
                return carry

            lax.fori_loop(0, c_groups, first_row, 0)
            lax.fori_loop(0, (n_blocks - 1) * c_groups, later_rows, 0)
        else:
            assert c_groups == 1
            def rows_iter(i, carry, br=br, r=r, span=span, u_n=u_n):
                specs = []
                for un in range(u_n):
                    n = i * u_n + un
                    if un == 0:
                        bias = bias_ref[jnp.minimum(n, 1)]
                        k_row = jnp.maximum(n - 1, 0) * span
                    else:
                        bias = bias_ref[1]
                        k_row = (n - 1) * span
                    specs += [(c + n * span, c + k_row, 2 * ATT_BLOCK, bias) for c in range(r)]
                blocks(br, r, specs)
                return carry

            lax.fori_loop(0, n_blocks // u_n, rows_iter, 0)

    def merge(i, carry):
        rs = pl.ds(pl.multiple_of(i * ATT_MERGE_ROWS, ATT_MERGE_ROWS), ATT_MERGE_ROWS)
        lses = [lse_ref[b, rs, :] for b in range(len(DIL_CONFIGS))]
        m = functools.reduce(jnp.maximum, lses)
        ws = [jnp.exp2(x - m) for x in lses]
        den = functools.reduce(lambda a, b: a + b, ws)
        num = functools.reduce(lambda a, b: a + b, [w * ob_ref[b, rs, :] for b, w in enumerate(ws)])
        o_ref[0, rs, :] = (num / den).astype(o_ref.dtype)
        return carry

    lax.fori_loop(0, seq // ATT_MERGE_ROWS, merge, 0)


def _dil_attn(z3):
    bsz, seq, _ = z3.shape
    bias, ones = _att_consts()
    nb = len(DIL_CONFIGS)
    return pl.pallas_call(
        functools.partial(_att_body, seq=seq),
        grid=(bsz, ATT_HEADS),
        in_specs=[
            pl.BlockSpec((1, seq, ATT_HD), lambda b, h: (b, 0, COL_QA // ATT_HD + h)),
            pl.BlockSpec((1, seq, ATT_HD), lambda b, h: (b, 0, COL_KA // ATT_HD + h)),
            pl.BlockSpec((1, seq, ATT_HD), lambda b, h: (b, 0, COL_VA // ATT_HD + h)),
            pl.BlockSpec((2, ATT_BLOCK, 2 * ATT_BLOCK), lambda b, h: (0, 0, 0)),
            pl.BlockSpec((2 * ATT_BLOCK, LANES), lambda b, h: (0, 0)),
        ],
        out_specs=pl.BlockSpec((1, seq, ATT_HD), lambda b, h: (b, 0, h)),
        out_shape=jax.ShapeDtypeStruct((bsz, seq, ATT_WIDTH), BF16),
        scratch_shapes=[pltpu.VMEM((3, seq, ATT_HD), F32),
                        pltpu.VMEM((nb, seq, ATT_HD), F32),
                        pltpu.VMEM((nb, seq, ATT_HD), F32)],
        compiler_params=_params(("parallel", "parallel")),
        name="dil_attn",
    )(z3, z3, z3, bias, ones)


OUT_TM = 512
OUT_TN = D_MODEL


def _out_body(x_ref, og_ref, oa_ref, wg_ref, wa_ref, o_ref):
    o_ref[...] = x_ref[...] + _dot(og_ref[...], wg_ref[...]) + _dot(oa_ref[...], wa_ref[...])


def _out_proj(x2d, og, oa, w_top, w_bot):
    t = x2d.shape[0]
    return pl.pallas_call(
        _out_body,
        grid=(t // OUT_TM, D_MODEL // OUT_TN),
        in_specs=[
            pl.BlockSpec((OUT_TM, OUT_TN), lambda i, j: (i, j)),
            pl.BlockSpec((OUT_TM, GLA_WIDTH), lambda i, j: (i, 0)),
            pl.BlockSpec((OUT_TM, ATT_WIDTH), lambda i, j: (i, 0)),
            pl.BlockSpec((GLA_WIDTH, OUT_TN), lambda i, j: (0, j)),
            pl.BlockSpec((ATT_WIDTH, OUT_TN), lambda i, j: (0, j)),
        ],
        out_specs=pl.BlockSpec((OUT_TM, OUT_TN), lambda i, j: (i, j)),
        out_shape=jax.ShapeDtypeStruct((t, D_MODEL), F32),
        compiler_params=_params(("parallel", "arbitrary")),
        name="out_proj",
    )(x2d, og, oa, w_top, w_bot)


def _rope_tables(seq):
    half = ATT_HD // 2
    inv_freq = 1.0 / (ROPE_THETA ** (np.arange(half, dtype=np.float64) / half))
    ang = np.arange(seq, dtype=np.float64)[:, None] * inv_freq[None, :]
    cos, sin = np.cos(ang), np.sin(ang)
    return (jnp.asarray(np.concatenate([cos, cos], axis=1), F32),
            jnp.asarray(np.concatenate([-sin, sin], axis=1), F32))


def kernel(x, ffn1_norm, ffn1_w_gate, ffn1_w_up, ffn1_w_down, mix_norm, w_in, gla_gate_up, gla_gate_bias, gla_out_norm, att_q_norm, att_k_norm, w_out, ffn2_norm, ffn2_w_gate, ffn2_w_up, ffn2_w_down):
    bsz, seq, d = x.shape
    depth = ffn1_norm.shape[0]
    t = bsz * seq
    cos, sin = _rope_tables(seq)
    x2d = x.reshape(t, d)
    for l in range(depth):
        x2d, w2_gate, w2_up, w2_down, w_main, w_low, w_o = _ffn(
            x2d, ffn1_norm[l][None], ffn1_w_gate[l].astype(BF16), ffn1_w_up[l].astype(BF16),
            ffn1_w_down[l].astype(BF16),
            prepare=(ffn2_w_gate[l], ffn2_w_up[l], ffn2_w_down[l], w_in[l].astype(BF16), w_out[l]))

        gate_up = jnp.pad(gla_gate_up[l], ((0, LANES - GLA_GATE_RANK), (0, 0))).astype(BF16)
        z, la = _in_proj(x2d, mix_norm[l][None], w_main, w_low, gate_up, gla_gate_bias[l][None],
                         att_q_norm[l][None], att_k_norm[l][None], cos, sin, seq)
        z3 = z.reshape(bsz, seq, D_MAIN)
        o_g = _gla(z3, la.reshape(bsz, seq, GLA_QK), gla_out_norm[l][None])
        o_a = _dil_attn(z3)

        x2d = _out_proj(x2d, o_g.reshape(t, GLA_WIDTH), o_a.reshape(t, ATT_WIDTH),
                        w_o[:GLA_WIDTH], w_o[GLA_WIDTH:])

        x2d = _ffn(x2d, ffn2_norm[l][None], w2_gate, w2_up, w2_down)
    return x2d.reshape(bsz, seq, d)
```
